```python
import functools
import jax, jax.numpy as jnp
from jax import lax
import numpy as np

D_MODEL = 1024
BATCH = 1
SEQ = 16384
DEPTH = 1
DEC_BATCH = 128
DEC_SEQ = 1
PAST_LEN = 16384
PAGE_SIZE = 128

D_RNN = D_MODEL
RNN_BLOCKS = 16
RNN_BW = D_RNN // RNN_BLOCKS
CONV_W = 4
LRU_C = 8.0
N_HEADS = 8
QK_NOPE = 128
QK_ROPE = 64
V_DIM = D_MODEL // N_HEADS
Q_LORA = 384
KV_LORA = 256
ROPE_THETA = 10000.0
SM_SCALE = (QK_NOPE + QK_ROPE) ** -0.5
Q_BLOCK = 128
D_FF = 4 * D_MODEL
EPS = 1e-6
NEG = -1e30
SPLITS = [D_RNN, D_RNN + Q_LORA, D_RNN + Q_LORA + KV_LORA, D_RNN + Q_LORA + KV_LORA + QK_ROPE,
          D_RNN + Q_LORA + KV_LORA + QK_ROPE + D_MODEL]
D_IN = D_RNN + Q_LORA + KV_LORA + QK_ROPE + 2 * D_MODEL

kernel_name = 'griffin_mla_hybrid_step'


def rmsnorm(x, g):
    xf = x.astype(jnp.float32)
    y = xf * lax.rsqrt(jnp.mean(xf * xf, axis=-1, keepdims=True) + EPS)
    return (y * g.astype(jnp.float32)).astype(x.dtype)


def rope(x, pos):
    half = x.shape[-1] // 2
    inv = 1.0 / (ROPE_THETA ** (jnp.arange(half, dtype=jnp.float32) / half))
    ang = pos.astype(jnp.float32)[:, None] * inv[None, :]
    shape = (pos.shape[0],) + (1,) * (x.ndim - 3) + (half,)
    cos = jnp.cos(ang).reshape(shape)
    sin = jnp.sin(ang).reshape(shape)
    xf = x.astype(jnp.float32)
    x1, x2 = xf[..., :half], xf[..., half:]
    return jnp.concatenate([x1 * cos - x2 * sin, x2 * cos + x1 * sin], axis=-1).astype(x.dtype)


def causal_conv(x, prev, w, b):
    t = x.shape[1]
    xp = jnp.concatenate([prev, x], axis=1)
    y = b
    for k in range(CONV_W):
        y = y + xp[:, k:k + t] * w[k]
    return y, xp[:, -(CONV_W - 1):]


def lru_combine(left, right):
    a1, b1 = left
    a2, b2 = right
    return a1 * a2, a2 * b1 + b2


def rglru(x, h0, pos, w_gate_a, b_gate_a, w_gate_x, b_gate_x, lru_lambda):
    bsz, t, _ = x.shape
    xb = x.reshape(bsz, t, RNN_BLOCKS, RNN_BW)
    r = jax.nn.sigmoid(jnp.einsum('btni,nij->btnj', xb, w_gate_a).reshape(bsz, t, D_RNN) + b_gate_a)
    i = jax.nn.sigmoid(jnp.einsum('btni,nij->btnj', xb, w_gate_x).reshape(bsz, t, D_RNN) + b_gate_x)
    log_a = -LRU_C * r.astype(jnp.float32) * jax.nn.softplus(-lru_lambda.astype(jnp.float32))
    a = jnp.exp(log_a)
    mult = jnp.sqrt(-jnp.expm1(2.0 * log_a))
    mult = jnp.where(pos[None, :, None] == 0, 1.0, mult)
    b = mult * (i * x).astype(jnp.float32)
    b = b.at[:, 0].add(a[:, 0] * h0.astype(jnp.float32))
    _, h = lax.associative_scan(lru_combine, (a, b), axis=1)
    h = h.astype(x.dtype)
    return h, h[:, -1]


def mla_prompt_attention(q_lat, q_pe, lat, kpe):
    bsz, s_len = q_lat.shape[:2]
    nb = s_len // Q_BLOCK
    ql = q_lat.reshape(bsz, nb, Q_BLOCK, N_HEADS, KV_LORA).swapaxes(0, 1)
    qp = q_pe.reshape(bsz, nb, Q_BLOCK, N_HEADS, QK_ROPE).swapaxes(0, 1)
    kpos = jnp.arange(s_len)

    def one_block(args):
        qlb, qpb, start = args
        s = (jnp.einsum('bqhr,bkr->bhqk', qlb, lat, preferred_element_type=jnp.float32)
             + jnp.einsum('bqhp,bkp->bhqk', qpb, kpe, preferred_element_type=jnp.float32)) * SM_SCALE
        qpos = start + jnp.arange(Q_BLOCK)
        s = jnp.where(kpos[None, :] <= qpos[:, None], s, NEG)
        p = jax.nn.softmax(s, axis=-1)
        return jnp.einsum('bhqk,bkr->bqhr', p.astype(lat.dtype), lat)

    o = lax.map(one_block, (ql, qp, jnp.arange(nb) * Q_BLOCK))
    return o.swapaxes(0, 1).reshape(bsz, s_len, N_HEADS, KV_LORA)


def mla_sample_attention(q_lat, q_pe, lat, kpe, pool_lat, pool_kpe, page_table):
    bd, t = q_lat.shape[:2]

    def scores(kl, kp):
        return (jnp.einsum('bthr,bkr->bthk', q_lat, kl, preferred_element_type=jnp.float32)
                + jnp.einsum('bthp,bkp->bthk', q_pe, kp, preferred_element_type=jnp.float32)) * SM_SCALE

    def page_step(carry, pages):
        m, l, acc = carry
        kl = pool_lat[pages]
        kp = pool_kpe[pages]
        s = scores(kl, kp)
        m_new = jnp.maximum(m, s.max(axis=-1))
        corr = jnp.exp(m - m_new)
        p = jnp.exp(s - m_new[..., None])
        acc = acc * corr[..., None] + jnp.einsum('bthk,bkr->bthr', p, kl.astype(jnp.float32))
        return (m_new, l * corr + p.sum(axis=-1), acc), None

    init = (jnp.full((bd, t, N_HEADS), NEG, jnp.float32),
            jnp.zeros((bd, t, N_HEADS), jnp.float32),
            jnp.zeros((bd, t, N_HEADS, KV_LORA), jnp.float32))
    (m, l, acc), _ = lax.scan(page_step, init, page_table.T)
    s = scores(lat, kpe)
    causal = jnp.arange(t)[None, :] <= jnp.arange(t)[:, None]
    s = jnp.where(causal[None, :, None, :], s, NEG)
    m_new = jnp.maximum(m, s.max(axis=-1))
    corr = jnp.exp(m - m_new)
    p = jnp.exp(s - m_new[..., None])
    acc = acc * corr[..., None] + jnp.einsum('bthk,bkr->bthr', p, lat.astype(jnp.float32))
    l = l * corr + p.sum(axis=-1)
    return (acc / l[..., None]).astype(q_lat.dtype)


def trunk_layer(x, c, pos, conv_prev, h_prev, attend, wl):
    (w_ada, b_ada, g_pre_mix, g_post_mix, g_pre_mlp, g_post_mlp, w_in, w_conv, b_conv,
     w_gate_a, b_gate_a, w_gate_x, b_gate_x, lru_lambda, g_q, w_uq, g_kv, w_uk, w_uv,
     w_o, w_up, w_down) = wl
    bsz, t, _ = x.shape
    mod = jax.nn.silu(c) @ w_ada + b_ada
    sh1, sc1, gt1, sh2, sc2, gt2 = [m[:, None, :] for m in jnp.split(mod, 6, axis=-1)]
    h = rmsnorm(x, g_pre_mix) * (1.0 + sc1) + sh1
    z = h @ w_in
    xr, cq, ckv, kpe_raw, g_rnn, g_att = jnp.split(z, SPLITS, axis=-1)
    xc, conv_new = causal_conv(xr, conv_prev, w_conv, b_conv)
    y_rnn, h_new = rglru(xc, h_prev, pos, w_gate_a, b_gate_a, w_gate_x, b_gate_x, lru_lambda)
    q = jnp.einsum('btq,qhd->bthd', rmsnorm(cq, g_q), w_uq)
    q_pe = rope(q[..., QK_NOPE:], pos)
    q_lat = jnp.einsum('bthn,rhn->bthr', q[..., :QK_NOPE], w_uk)
    lat = rmsnorm(ckv, g_kv)
    kpe = rope(kpe_raw, pos)
    o_lat = attend(q_lat, q_pe, lat, kpe)
    y_att = jnp.einsum('bthr,rhv->bthv', o_lat, w_uv).reshape(bsz, t, N_HEADS * V_DIM)
    merged = jax.nn.sigmoid(g_rnn) * y_rnn + jax.nn.sigmoid(g_att) * y_att
    x = x + gt1 * rmsnorm(merged @ w_o, g_post_mix)
    h2 = rmsnorm(x, g_pre_mlp) * (1.0 + sc2) + sh2
    f = jnp.square(jax.nn.relu(h2 @ w_up)) @ w_down
    x = x + gt2 * rmsnorm(f, g_post_mlp)
    return x, lat, kpe, conv_new, h_new


def setup_inputs(seed: int = 0) -> dict:
    key = jax.random.key(seed)
    ks = jax.random.split(key, 40)
    f32 = jnp.float32
    n_pages = PAST_LEN // PAGE_SIZE
    n_used = DEC_BATCH * n_pages
    n_pool = n_used + max(1, n_used // 4)

    def nrm(k, shape, scale):
        return jax.random.normal(k, shape, f32) * scale

    def gain(k, n):
        return 1.0 + 0.05 * jax.random.normal(k, (DEPTH, n), f32)

    a0 = jax.random.uniform(ks[30], (DEPTH, D_RNN), f32, 0.9, 0.999)
    s0 = a0 ** (1.0 / LRU_C)
    lru_lambda = jnp.log(s0) - jnp.log1p(-s0)
    page_table = jax.random.permutation(ks[8], n_pool)[:n_used].reshape(DEC_BATCH, n_pages).astype(jnp.int32)
    return {
        'x_prompt': nrm(ks[0], (BATCH, SEQ, D_MODEL), 1.0),
        'x_sample': nrm(ks[1], (DEC_BATCH, DEC_SEQ, D_MODEL), 1.0),
        'c_prompt': nrm(ks[2], (BATCH, D_MODEL), 1.0),
        'c_sample': nrm(ks[3], (DEC_BATCH, D_MODEL), 1.0),
        'cache_latent': nrm(ks[4], (DEPTH, n_pool, PAGE_SIZE, KV_LORA), 1.0),
        'cache_krope': nrm(ks[5], (DEPTH, n_pool, PAGE_SIZE, QK_ROPE), 1.0),
        'state_conv': nrm(ks[6], (DEPTH, DEC_BATCH, CONV_W - 1, D_RNN), 1.0),
        'state_rnn': nrm(ks[7], (DEPTH, DEC_BATCH, D_RNN), 0.5),
        'page_table': page_table,
        'w_ada': nrm(ks[9], (DEPTH, D_MODEL, 6 * D_MODEL), 0.5 * D_MODEL ** -0.5),
        'b_ada': nrm(ks[10], (DEPTH, 6 * D_MODEL), 0.01),
        'g_pre_mix': gain(ks[11], D_MODEL),
        'g_post_mix': gain(ks[12], D_MODEL),
        'g_pre_mlp': gain(ks[13], D_MODEL),
        'g_post_mlp': gain(ks[14], D_MODEL),
        'w_in': nrm(ks[15], (DEPTH, D_MODEL, D_IN), D_MODEL ** -0.5),
        'w_conv': nrm(ks[16], (DEPTH, CONV_W, D_RNN), CONV_W ** -0.5),
        'b_conv': nrm(ks[17], (DEPTH, D_RNN), 0.01),
        'w_gate_a': nrm(ks[18], (DEPTH, RNN_BLOCKS, RNN_BW, RNN_BW), RNN_BW ** -0.5),
        'b_gate_a': nrm(ks[19], (DEPTH, D_RNN), 0.01),
        'w_gate_x': nrm(ks[20], (DEPTH, RNN_BLOCKS, RNN_BW, RNN_BW), RNN_BW ** -0.5),
        'b_gate_x': nrm(ks[21], (DEPTH, D_RNN), 0.01),
        'lru_lambda': lru_lambda,
        'g_q': gain(ks[22], Q_LORA),
        'w_uq': nrm(ks[23], (DEPTH, Q_LORA, N_HEADS, QK_NOPE + QK_ROPE), Q_LORA ** -0.5),
        'g_kv': gain(ks[24], KV_LORA),
        'w_uk': nrm(ks[25], (DEPTH, KV_LORA, N_HEADS, QK_NOPE), QK_NOPE ** -0.5),
        'w_uv': nrm(ks[26], (DEPTH, KV_LORA, N_HEADS, V_DIM), KV_LORA ** -0.5),
        'w_o': nrm(ks[27], (DEPTH, D_MODEL, D_MODEL), D_MODEL ** -0.5),
        'w_up': nrm(ks[28], (DEPTH, D_MODEL, D_FF), D_MODEL ** -0.5),
        'w_down': nrm(ks[29], (DEPTH, D_FF, D_MODEL), D_FF ** -0.5),
    }


def reference(x_prompt, x_sample, c_prompt, c_sample, cache_latent, cache_krope, state_conv, state_rnn,
              page_table, w_ada, b_ada, g_pre_mix, g_post_mix, g_pre_mlp, g_post_mlp, w_in, w_conv, b_conv,
              w_gate_a, b_gate_a, w_gate_x, b_gate_x, lru_lambda, g_q, w_uq, g_kv, w_uk, w_uv,
              w_o, w_up, w_down):
    pos_p = jnp.arange(x_prompt.shape[1])
    pos_s = PAST_LEN + jnp.arange(x_sample.shape[1])
    xp, xs = x_prompt, x_sample
    lat_p, kpe_p, conv_p, rnn_p = [], [], [], []
    lat_s, kpe_s, conv_s, rnn_s = [], [], [], []
    for l in range(DEPTH):
        wl = (w_ada[l], b_ada[l], g_pre_mix[l], g_post_mix[l], g_pre_mlp[l], g_post_mlp[l], w_in[l],
              w_conv[l], b_conv[l], w_gate_a[l], b_gate_a[l], w_gate_x[l], b_gate_x[l], lru_lambda[l],
              g_q[l], w_uq[l], g_kv[l], w_uk[l], w_uv[l], w_o[l], w_up[l], w_down[l])
        conv0 = jnp.zeros((xp.shape[0], CONV_W - 1, D_RNN), xp.dtype)
        h0 = jnp.zeros((xp.shape[0], D_RNN), xp.dtype)
        xp, la, kp, cv, hr = trunk_layer(xp, c_prompt, pos_p, conv0, h0, mla_prompt_attention, wl)
        lat_p.append(la)
        kpe_p.append(kp)
        conv_p.append(cv)
        rnn_p.append(hr)
        attend_s = functools.partial(mla_sample_attention, pool_lat=cache_latent[l],
                                     pool_kpe=cache_krope[l], page_table=page_table)
        xs, la, kp, cv, hr = trunk_layer(xs, c_sample, pos_s, state_conv[l], state_rnn[l], attend_s, wl)
        lat_s.append(la)
        kpe_s.append(kp)
        conv_s.append(cv)
        rnn_s.append(hr)
    return (xp, xs,
            jnp.stack(lat_p), jnp.stack(kpe_p), jnp.stack(conv_p), jnp.stack(rnn_p),
            jnp.stack(lat_s), jnp.stack(kpe_s), jnp.stack(conv_s), jnp.stack(rnn_s))
```

```python
import functools

import jax
import jax.numpy as jnp
from jax import lax
from jax.experimental import pallas as pl
from jax.experimental.pallas import tpu as pltpu

D_MODEL = 1024
D_RNN = 1024
RNN_BLOCKS = 16
RNN_BW = D_RNN // RNN_BLOCKS
CONV_W = 4
LRU_C = 8.0
N_HEADS = 8
QK_NOPE = 128
QK_ROPE = 64
V_DIM = D_MODEL // N_HEADS
Q_LORA = 384
KV_LORA = 256
ROPE_THETA = 10000.0
SM_SCALE = (QK_NOPE + QK_ROPE) ** -0.5
D_FF = 4 * D_MODEL
EPS = 1e-6
NEG = -1e30
PAGE_SIZE = 128

LANES = 128
MXU_DIM = 256
VMEM_LIMIT = 56 * 1024 * 1024

Z_XR = 0
Z_CQ = Z_XR + D_RNN
Z_CKV = Z_CQ + Q_LORA
Z_GR = Z_CKV + KV_LORA
Z_GA = Z_GR + D_MODEL
Z_KPE = Z_GA + D_MODEL
Z_END = Z_KPE + LANES

BF16 = jnp.bfloat16
F32 = jnp.float32


def _rms(x, g):
    return x * lax.rsqrt(jnp.mean(x * x, axis=-1, keepdims=True) + EPS) * g


def _dot(a, b):
    return jnp.dot(a, b, preferred_element_type=F32)


def _dot_nt(a, b):
    return lax.dot_general(a, b, (((1,), (1,)), ((), ())), preferred_element_type=F32)


def _const_spec(shape):
    zeros = (0,) * len(shape)
    return pl.BlockSpec(shape, lambda *_: zeros, pipeline_mode=pl.Buffered(1))


def _params(n_axes=1):
    return pltpu.CompilerParams(dimension_semantics=("arbitrary",) * n_axes, vmem_limit_bytes=VMEM_LIMIT)


def _ada_kernel(c_ref, w_ref, b_ref, o_ref):
    c = c_ref[...]
    o_ref[...] = _dot((c * jax.nn.sigmoid(c)).astype(BF16), w_ref[...]) + b_ref[...]


def _ada(c_all, w_ada, b_ada):
    m = c_all.shape[0]
    tn = D_MODEL
    return pl.pallas_call(
        _ada_kernel,
        grid=(6 * D_MODEL // tn,),
        in_specs=[pl.BlockSpec((m, D_MODEL), lambda j: (0, 0)),
                  pl.BlockSpec((D_MODEL, tn), lambda j: (0, j)),
                  pl.BlockSpec((1, tn), lambda j: (0, j))],
        out_specs=pl.BlockSpec((m, tn), lambda j: (0, j)),
        out_shape=jax.ShapeDtypeStruct((m, 6 * D_MODEL), F32),
        compiler_params=_params(),
        name="ada",
    )(c_all, w_ada, b_ada)


def _swap_halves_64(x):
    n = x.shape[-1]
    lane = lax.broadcasted_iota(jnp.int32, x.shape, x.ndim - 1)
    from_hi = pltpu.roll(x, n - QK_ROPE // 2, x.ndim - 1)
    from_lo = pltpu.roll(x, QK_ROPE // 2, x.ndim - 1)
    return jnp.where(lane % QK_ROPE < QK_ROPE // 2, from_hi, from_lo)


def _inproj_kernel(x_ref, sh_ref, sc_ref, gpre_ref, win_ref, gq_ref, wuq_ref, wuk_ref, gkv_ref,
                   cq_ref, sq_ref, ck_ref, sk_ref,
                   xr_ref, gr_ref, ga_ref, qlat_ref, qpe_ref, lat_ref, latb_ref, kpe_ref, kpeb_ref):
    x = x_ref[...]
    h = _rms(x, gpre_ref[...]) * (1.0 + sc_ref[...]) + sh_ref[...]
    z = _dot(h.astype(BF16), win_ref[...])
    xr_ref[...] = z[:, Z_XR:Z_CQ]
    gr_ref[...] = z[:, Z_GR:Z_GA]
    ga_ref[...] = z[:, Z_GA:Z_KPE]

    q = _dot(_rms(z[:, Z_CQ:Z_CKV], gq_ref[...]).astype(BF16), wuq_ref[...])
    q_rope = q[:, N_HEADS * QK_NOPE:]
    q_rope = (q_rope * cq_ref[...] + _swap_halves_64(q_rope) * sq_ref[...]) * SM_SCALE
    for hd in range(N_HEADS):
        q_nope = q[:, hd * QK_NOPE:(hd + 1) * QK_NOPE].astype(BF16)
        qlat_ref[hd] = (_dot(q_nope, wuk_ref[hd]) * SM_SCALE).astype(BF16)
        qpe_ref[hd] = q_rope[:, hd * QK_ROPE:(hd + 1) * QK_ROPE].astype(BF16)

    lat = _rms(z[:, Z_CKV:Z_GR], gkv_ref[...])
    lat_ref[...] = lat
    latb_ref[...] = lat.astype(BF16)
    k_raw = z[:, Z_KPE:Z_END]
    kpe = (k_raw * ck_ref[...] + _swap_halves_64(k_raw) * sk_ref[...])[:, :QK_ROPE]
    kpe_ref[...] = kpe
    kpeb_ref[...] = kpe.astype(BF16)


def _inproj(x, sh, sc, w, rope_tabs, tm):
    m = x.shape[0]
    per_row = sh.shape[0] != 1
    row = lambda i: (i, 0)
    mod_spec = pl.BlockSpec((tm, D_MODEL), row) if per_row else _const_spec((1, D_MODEL))
    cq, sq, ck, sk = rope_tabs
    in_specs = [
        pl.BlockSpec((tm, D_MODEL), row), mod_spec, mod_spec,
        _const_spec((1, D_MODEL)), _const_spec((D_MODEL, Z_END)), _const_spec((1, Q_LORA)),
        _const_spec((Q_LORA, N_HEADS * (QK_NOPE + QK_ROPE))), _const_spec((N_HEADS, QK_NOPE, KV_LORA)),
        _const_spec((1, KV_LORA)),
        pl.BlockSpec((tm, N_HEADS * QK_ROPE), row), pl.BlockSpec((tm, N_HEADS * QK_ROPE), row),
        pl.BlockSpec((tm, LANES), row), pl.BlockSpec((tm, LANES), row),
    ]
    out_shape = [
        jax.ShapeDtypeStruct((m, D_RNN), F32), jax.ShapeDtypeStruct((m, D_MODEL), F32),
        jax.ShapeDtypeStruct((m, D_MODEL), F32),
        jax.ShapeDtypeStruct((N_HEADS, m, KV_LORA), BF16), jax.ShapeDtypeStruct((N_HEADS, m, QK_ROPE), BF16),
        jax.ShapeDtypeStruct((m, KV_LORA), F32), jax.ShapeDtypeStruct((m, KV_LORA), BF16),
        jax.ShapeDtypeStruct((m, QK_ROPE), F32), jax.ShapeDtypeStruct((m, QK_ROPE), BF16),
    ]
    out_specs = [
        pl.BlockSpec((tm, D_RNN), row), pl.BlockSpec((tm, D_MODEL), row), pl.BlockSpec((tm, D_MODEL), row),
        pl.BlockSpec((N_HEADS, tm, KV_LORA), lambda i: (0, i, 0)),
        pl.BlockSpec((N_HEADS, tm, QK_ROPE), lambda i: (0, i, 0)),
        pl.BlockSpec((tm, KV_LORA), row), pl.BlockSpec((tm, KV_LORA), row),
        pl.BlockSpec((tm, QK_ROPE), row), pl.BlockSpec((tm, QK_ROPE), row),
    ]
    return pl.pallas_call(
        _inproj_kernel, grid=(m // tm,), in_specs=in_specs, out_specs=out_specs, out_shape=out_shape,
        compiler_params=_params(), name="inproj",
    )(x, sh, sc, w["g_pre_mix"], w["w_in"], w["g_q"], w["w_uq"], w["w_uk"], w["g_kv"], cq, sq, ck, sk)


def _lru_coeffs(xc, wa_ref, ba_ref, wx_ref, bx_ref, lam_ref):
    xb = xc.astype(BF16)
    r_parts, i_parts = [], []
    for j in range(D_RNN // MXU_DIM):
        blk = xb[:, j * MXU_DIM:(j + 1) * MXU_DIM]
        r_parts.append(_dot(blk, wa_ref[j]))
        i_parts.append(_dot(blk, wx_ref[j]))
    r = jax.nn.sigmoid(jnp.concatenate(r_parts, axis=1) + ba_ref[...])
    i = jax.nn.sigmoid(jnp.concatenate(i_parts, axis=1) + bx_ref[...])
    lam = lam_ref[...]
    softplus_neg_lam = jnp.maximum(-lam, 0.0) + jnp.log1p(jnp.exp(-jnp.abs(lam)))
    log_a = -LRU_C * r * softplus_neg_lam
    a = jnp.exp(log_a)
    mult = jnp.sqrt(-jnp.tanh(log_a) * (a * a + 1.0))
    return a, mult, i * xc


def _rglru_prompt_kernel(xr_ref, g_ref, wc_ref, bc_ref, wa_ref, ba_ref, wx_ref, bx_ref, lam_ref,
                         yg_ref, hlast_ref, xp_scr, h_scr, *, tm):
    step = pl.program_id(0)

    @pl.when(step == 0)
    def _():
        xp_scr[0:8, :] = jnp.zeros((8, D_RNN), F32)
        h_scr[...] = jnp.zeros((1, D_RNN), F32)

    x = xr_ref[...]
    xp_scr[8:8 + tm, :] = x
    wc = wc_ref[...]
    xc = (bc_ref[...] + x * wc[3:4] + xp_scr[7:7 + tm, :] * wc[2:3]
          + xp_scr[6:6 + tm, :] * wc[1:2] + xp_scr[5:5 + tm, :] * wc[0:1])
    xp_scr[0:8, :] = xp_scr[tm:tm + 8, :]

    a, mult, gated = _lru_coeffs(xc, wa_ref, ba_ref, wx_ref, bx_ref, lam_ref)
    rows = lax.broadcasted_iota(jnp.int32, (tm, D_RNN), 0)
    mult = jnp.where(rows + step * tm == 0, 1.0, mult)
    b = mult * gated

    d = 1
    while d < tm:
        keep = rows >= d
        b = jnp.where(keep, a * pltpu.roll(b, d, 0) + b, b)
        a = jnp.where(keep, a * pltpu.roll(a, d, 0), a)
        d *= 2
    h = a * h_scr[...] + b
    h_scr[...] = h[tm - 1:tm, :]
    hlast_ref[...] = h[tm - 1:tm, :]
    yg_ref[...] = jax.nn.sigmoid(g_ref[...]) * h


def _rglru_prompt(xr, g_rnn, w, tm):
    m = xr.shape[0]
    row = lambda i: (i, 0)
    nblk = D_RNN // MXU_DIM
    in_specs = [pl.BlockSpec((tm, D_RNN), row), pl.BlockSpec((tm, D_MODEL), row),
                _const_spec((CONV_W, D_RNN)), _const_spec((1, D_RNN)),
                _const_spec((nblk, MXU_DIM, MXU_DIM)), _const_spec((1, D_RNN)),
                _const_spec((nblk, MXU_DIM, MXU_DIM)), _const_spec((1, D_RNN)), _const_spec((1, D_RNN))]
    return pl.pallas_call(
        functools.partial(_rglru_prompt_kernel, tm=tm), grid=(m // tm,), in_specs=in_specs,
        out_specs=[pl.BlockSpec((tm, D_MODEL), row), pl.BlockSpec((1, D_RNN), lambda i: (0, 0))],
        out_shape=[jax.ShapeDtypeStruct((m, D_MODEL), F32), jax.ShapeDtypeStruct((1, D_RNN), F32)],
        scratch_shapes=[pltpu.VMEM((tm + 8, D_RNN), F32), pltpu.VMEM((1, D_RNN), F32)],
        compiler_params=_params(), name="rglru_prompt",
    )(xr, g_rnn, w["w_conv"], w["b_conv"], w["w_gate_a"], w["b_gate_a"], w["w_gate_x"], w["b_gate_x"],
      w["lru_lambda"])


def _rglru_sample_kernel(xr_ref, g_ref, s0_ref, s1_ref, s2_ref, h0_ref, wc_ref, bc_ref, wa_ref, ba_ref,
                         wx_ref, bx_ref, lam_ref, yg_ref, h_ref, *, at_start):
    wc = wc_ref[...]
    xc = (bc_ref[...] + xr_ref[...] * wc[3:4] + s2_ref[...] * wc[2:3] + s1_ref[...] * wc[1:2]
          + s0_ref[...] * wc[0:1])
    a, mult, gated = _lru_coeffs(xc, wa_ref, ba_ref, wx_ref, bx_ref, lam_ref)
    if at_start:
        mult = jnp.ones_like(mult)
    h = a * h0_ref[...] + mult * gated
    h_ref[...] = h
    yg_ref[...] = jax.nn.sigmoid(g_ref[...]) * h


def _rglru_sample(xr, g_rnn, conv_rows, h0, w, at_start):
    m = xr.shape[0]
    nblk = D_RNN // MXU_DIM
    full = _const_spec((m, D_RNN))
    in_specs = [full] * 6 + [_const_spec((CONV_W, D_RNN)), _const_spec((1, D_RNN)),
                             _const_spec((nblk, MXU_DIM, MXU_DIM)), _const_spec((1, D_RNN)),
                             _const_spec((nblk, MXU_DIM, MXU_DIM)), _const_spec((1, D_RNN)),
                             _const_spec((1, D_RNN))]
    return pl.pallas_call(
        functools.partial(_rglru_sample_kernel, at_start=at_start), grid=(1,), in_specs=in_specs,
        out_specs=[full, full],
        out_shape=[jax.ShapeDtypeStruct((m, D_MODEL), F32), jax.ShapeDtypeStruct((m, D_RNN), F32)],
        compiler_params=_params(), name="rglru_sample",
    )(xr, g_rnn, *conv_rows, h0, w["w_conv"], w["b_conv"], w["w_gate_a"], w["b_gate_a"], w["w_gate_x"],
      w["b_gate_x"], w["lru_lambda"])


def _value_up_and_merge(o_of_head, ga_ref, yg_ref, wuv_ref, out_ref):
    for hd in range(N_HEADS):
        cols = slice(hd * V_DIM, (hd + 1) * V_DIM)
        y = _dot(o_of_head(hd).astype(BF16), wuv_ref[hd])
        out_ref[:, cols] = (jax.nn.sigmoid(ga_ref[:, cols]) * y + yg_ref[:, cols]).astype(out_ref.dtype)


def _attn_prompt_kernel(qlat_ref, qpe_ref, lat_ref, kpe_ref, ga_ref, yg_ref, wuv_ref, out_ref,
                        m_scr, l_scr, acc_scr, *, tq, tk):
    qi = pl.program_id(0)
    rows = N_HEADS * tq
    q_lat = qlat_ref[...].reshape(rows, KV_LORA)
    q_pe = qpe_ref[...].reshape(rows, QK_ROPE)
    m_scr[...] = jnp.full((rows, 1), NEG, F32)
    l_scr[...] = jnp.zeros((rows, 1), F32)
    acc_scr[...] = jnp.zeros((rows, KV_LORA), F32)

    def block(start, masked):
        kl = lat_ref[pl.ds(start, tk), :]
        kp = kpe_ref[pl.ds(start, tk), :]
        s = _dot_nt(q_lat, kl) + _dot_nt(q_pe, kp)
        if masked:
            qpos = qi * tq + lax.broadcasted_iota(jnp.int32, (rows, tk), 0) % tq
            kpos = start + lax.broadcasted_iota(jnp.int32, (rows, tk), 1)
            s = jnp.where(kpos <= qpos, s, NEG)
        m_prev = m_scr[...]
        m_new = jnp.maximum(m_prev, jnp.max(s, axis=1, keepdims=True))
        alpha = jnp.exp(m_prev - m_new)
        p = jnp.exp(s - m_new)
        l_scr[...] = alpha * l_scr[...] + jnp.sum(p, axis=1, keepdims=True)
        acc_scr[...] = alpha * acc_scr[...] + _dot(p.astype(BF16), kl)
        m_scr[...] = m_new

    n_full = (qi * tq) // tk

    def body(j, carry):
        block(pl.multiple_of(j * tk, tk), masked=False)
        return carry

    lax.fori_loop(0, n_full, body, 0)
    block(pl.multiple_of(n_full * tk, tk), masked=True)

    acc_scr[...] = acc_scr[...] / l_scr[...]
    _value_up_and_merge(lambda hd: acc_scr[hd * tq:(hd + 1) * tq, :], ga_ref, yg_ref, wuv_ref, out_ref)


def _attn_prompt(q_lat, q_pe, lat_b, kpe_b, g_att, yg, w, tq, tk):
    s_len = lat_b.shape[0]
    row = lambda i: (i, 0)
    in_specs = [pl.BlockSpec((N_HEADS, tq, KV_LORA), lambda i: (0, i, 0)),
                pl.BlockSpec((N_HEADS, tq, QK_ROPE), lambda i: (0, i, 0)),
                _const_spec((s_len, KV_LORA)), _const_spec((s_len, QK_ROPE)),
                pl.BlockSpec((tq, D_MODEL), row), pl.BlockSpec((tq, D_MODEL), row),
                _const_spec((N_HEADS, KV_LORA, V_DIM))]
    rows = N_HEADS * tq
    return pl.pallas_call(
        functools.partial(_attn_prompt_kernel, tq=tq, tk=tk), grid=(s_len // tq,), in_specs=in_specs,
        out_specs=pl.BlockSpec((tq, D_MODEL), row),
        out_shape=jax.ShapeDtypeStruct((s_len, D_MODEL), BF16),
        scratch_shapes=[pltpu.VMEM((rows, 1), F32), pltpu.VMEM((rows, 1), F32),
                        pltpu.VMEM((rows, KV_LORA), F32)],
        compiler_params=_params(), name="attn_prompt",
    )(q_lat, q_pe, lat_b, kpe_b, g_att, yg, w["w_uv"])


def _attn_sample_kernel(pt_ref, qlat_ref, qpe_ref, latn_ref, kpen_ref, *refs, pages):
    del pt_ref
    lat_refs, kpe_refs = refs[:pages], refs[pages:2 * pages]
    o_ref, m_scr, l_scr, acc_scr = refs[2 * pages:]
    c = pl.program_id(1)
    q_lat = qlat_ref[:, 0, :]
    q_pe = qpe_ref[:, 0, :]

    @pl.when(c == 0)
    def _():
        m_scr[...] = jnp.full((N_HEADS, 1), NEG, F32)
        l_scr[...] = jnp.zeros((N_HEADS, 1), F32)
        acc_scr[...] = jnp.zeros((N_HEADS, KV_LORA), F32)

    kls = [r[...].astype(BF16) for r in lat_refs]
    s = jnp.concatenate(
        [_dot_nt(q_lat, kl) + _dot_nt(q_pe, kr[...].astype(BF16)) for kl, kr in zip(kls, kpe_refs)], axis=1)
    m_prev = m_scr[...]
    m_new = jnp.maximum(m_prev, jnp.max(s, axis=1, keepdims=True))
    alpha = jnp.exp(m_prev - m_new)
    p = jnp.exp(s - m_new)
    pb = p.astype(BF16)
    pv = _dot(pb[:, :PAGE_SIZE], kls[0])
    for i in range(1, pages):
        pv += _dot(pb[:, i * PAGE_SIZE:(i + 1) * PAGE_SIZE], kls[i])
    l_new = alpha * l_scr[...] + jnp.sum(p, axis=1, keepdims=True)
    acc_new = alpha * acc_scr[...] + pv
    m_scr[...] = m_new
    l_scr[...] = l_new
    acc_scr[...] = acc_new

    @pl.when(c == pl.num_programs(1) - 1)
    def _():
        lat_n = latn_ref[...]
        s_n = (jnp.sum(q_lat.astype(F32) * lat_n, axis=1, keepdims=True)
               + jnp.sum(q_pe.astype(F32) * kpen_ref[...], axis=1, keepdims=True))
        m_fin = jnp.maximum(m_new, s_n)
        corr = jnp.exp(m_new - m_fin)
        p_n = jnp.exp(s_n - m_fin)
        o_ref[:, 0, :] = (acc_new * corr + p_n * lat_n) / (l_new * corr + p_n)


def _attn_sample(q_lat, q_pe, lat_new, kpe_new, pool_lat, pool_kpe, page_table, pages):
    bd, n_pages = page_table.shape
    n_chunks = n_pages // pages
    per_b = lambda b, c, pt: (0, b, 0, 0)

    def page_spec(width, i):
        return pl.BlockSpec((None, None, PAGE_SIZE, width), lambda b, c, pt: (0, pt[b, c * pages + i], 0, 0))

    in_specs = ([pl.BlockSpec((N_HEADS, None, 1, KV_LORA), per_b), pl.BlockSpec((N_HEADS, None, 1, QK_ROPE), per_b),
                 pl.BlockSpec((None, 1, KV_LORA), lambda b, c, pt: (b, 0, 0)),
                 pl.BlockSpec((None, 1, QK_ROPE), lambda b, c, pt: (b, 0, 0))]
                + [page_spec(KV_LORA, i) for i in range(pages)]
                + [page_spec(QK_ROPE, i) for i in range(pages)])
    grid_spec = pltpu.PrefetchScalarGridSpec(
        num_scalar_prefetch=1, grid=(bd, n_chunks), in_specs=in_specs,
        out_specs=pl.BlockSpec((N_HEADS, None, 1, KV_LORA), per_b),
        scratch_shapes=[pltpu.VMEM((N_HEADS, 1), F32), pltpu.VMEM((N_HEADS, 1), F32),
                        pltpu.VMEM((N_HEADS, KV_LORA), F32)])
    o = pl.pallas_call(
        functools.partial(_attn_sample_kernel, pages=pages), grid_spec=grid_spec,
        out_shape=jax.ShapeDtypeStruct((N_HEADS, bd, 1, KV_LORA), F32),
        compiler_params=_params(2), name="attn_sample",
    )(page_table, q_lat.reshape(N_HEADS, bd, 1, KV_LORA), q_pe.reshape(N_HEADS, bd, 1, QK_ROPE),
      lat_new.reshape(bd, 1, KV_LORA), kpe_new.reshape(bd, 1, QK_ROPE),
      *([pool_lat] * pages), *([pool_kpe] * pages))
    return o.reshape(N_HEADS, bd, KV_LORA)


def _merge_sample_kernel(o_ref, ga_ref, yg_ref, wuv_ref, out_ref):
    _value_up_and_merge(lambda hd: o_ref[hd], ga_ref, yg_ref, wuv_ref, out_ref)


def _merge_sample(o, g_att, yg, w):
    m = g_att.shape[0]
    return pl.pallas_call(
        _merge_sample_kernel, grid=(1,),
        in_specs=[_const_spec((N_HEADS, m, KV_LORA)), _const_spec((m, D_MODEL)), _const_spec((m, D_MODEL)),
                  _const_spec((N_HEADS, KV_LORA, V_DIM))],
        out_specs=_const_spec((m, D_MODEL)),
        out_shape=jax.ShapeDtypeStruct((m, D_MODEL), BF16),
        compiler_params=_params(), name="merge_sample",
    )(o, g_att, yg, w["w_uv"])


def _outmlp_kernel(mg_ref, x_ref, gt1_ref, sh2_ref, sc2_ref, gt2_ref, gpm_ref, gpre_ref, gpost_ref,
                   wo_ref, wup_ref, wdn_ref, y_ref, *, ff_chunk):
    x1 = x_ref[...] + gt1_ref[...] * _rms(_dot(mg_ref[...], wo_ref[...]), gpm_ref[...])
    h2 = (_rms(x1, gpre_ref[...]) * (1.0 + sc2_ref[...]) + sh2_ref[...]).astype(BF16)
    f = None
    for j in range(D_FF // ff_chunk):
        cols = slice(j * ff_chunk, (j + 1) * ff_chunk)
        u = jnp.maximum(_dot(h2, wup_ref[:, cols]), 0.0)
        part = _dot((u * u).astype(BF16), wdn_ref[cols, :])
        f = part if f is None else f + part
    y_ref[...] = x1 + gt2_ref[...] * _rms(f, gpost_ref[...])


def _outmlp(merged, x, gt1, sh2, sc2, gt2, w, tm, ff_chunk):
    m = x.shape[0]
    per_row = gt1.shape[0] != 1
    row = lambda i: (i, 0)
    mod_spec = pl.BlockSpec((tm, D_MODEL), row) if per_row else _const_spec((1, D_MODEL))
    vec = _const_spec((1, D_MODEL))
    in_specs = [pl.BlockSpec((tm, D_MODEL), row), pl.BlockSpec((tm, D_MODEL), row),
                mod_spec, mod_spec, mod_spec, mod_spec, vec, vec, vec,
                _const_spec((D_MODEL, D_MODEL)), _const_spec((D_MODEL, D_FF)), _const_spec((D_FF, D_MODEL))]
    return pl.pallas_call(
        functools.partial(_outmlp_kernel, ff_chunk=ff_chunk), grid=(m // tm,), in_specs=in_specs,
        out_specs=pl.BlockSpec((tm, D_MODEL), row),
        out_shape=jax.ShapeDtypeStruct((m, D_MODEL), F32),
        compiler_params=_params(), name="outmlp",
    )(merged, x, gt1, sh2, sc2, gt2, w["g_post_mix"], w["g_pre_mlp"], w["g_post_mlp"],
      w["w_o"], w["w_up"], w["w_down"])


def _rope_tables(pos):
    half = QK_ROPE // 2
    inv = 1.0 / (ROPE_THETA ** (jnp.arange(half, dtype=F32) / half))
    ang = pos.astype(F32)[:, None] * inv[None, :]
    cos, sin = jnp.cos(ang), jnp.sin(ang)
    c64 = jnp.concatenate([cos, cos], axis=1)
    s64 = jnp.concatenate([-sin, sin], axis=1)
    pad = jnp.zeros_like(c64)
    return (jnp.tile(c64, (1, N_HEADS)), jnp.tile(s64, (1, N_HEADS)),
            jnp.concatenate([c64, pad], axis=1), jnp.concatenate([s64, pad], axis=1))


def _block_diag_tiles(wg):
    per = MXU_DIM // RNN_BW
    wg = wg.reshape(D_RNN // MXU_DIM, per, RNN_BW, RNN_BW)
    eye = jnp.eye(per, dtype=wg.dtype)
    return jnp.einsum("tpij,pq->tpiqj", wg, eye).reshape(D_RNN // MXU_DIM, MXU_DIM, MXU_DIM).astype(BF16)


def _layer_weights(l, w_in, w_conv, b_conv, w_gate_a, b_gate_a, w_gate_x, b_gate_x, lru_lambda, g_q, w_uq,
                   g_kv, w_uk, w_uv, w_o, w_up, w_down, g_pre_mix, g_post_mix, g_pre_mlp, g_post_mlp):
    wi = w_in[l]
    s_xr, s_cq, s_ckv, s_kpe, s_gr = D_RNN, D_RNN + Q_LORA, D_RNN + Q_LORA + KV_LORA, \
        D_RNN + Q_LORA + KV_LORA + QK_ROPE, D_RNN + Q_LORA + KV_LORA + QK_ROPE + D_MODEL
    w_in_r = jnp.concatenate([wi[:, :s_ckv], wi[:, s_kpe:], wi[:, s_ckv:s_kpe],
                              jnp.zeros((D_MODEL, LANES - QK_ROPE), wi.dtype)], axis=1).astype(BF16)
    del s_xr, s_cq, s_gr
    uq = w_uq[l]
    w_uq_r = jnp.concatenate([uq[:, :, :QK_NOPE].reshape(Q_LORA, N_HEADS * QK_NOPE),
                              uq[:, :, QK_NOPE:].reshape(Q_LORA, N_HEADS * QK_ROPE)], axis=1).astype(BF16)
    vec = lambda v: v[l].reshape(1, -1)
    return {
        "w_in": w_in_r, "w_uq": w_uq_r,
        "w_uk": jnp.transpose(w_uk[l], (1, 2, 0)).astype(BF16),
        "w_uv": jnp.transpose(w_uv[l], (1, 0, 2)).astype(BF16),
        "w_o": w_o[l].astype(BF16), "w_up": w_up[l].astype(BF16), "w_down": w_down[l].astype(BF16),
        "w_gate_a": _block_diag_tiles(w_gate_a[l]), "w_gate_x": _block_diag_tiles(w_gate_x[l]),
        "w_conv": w_conv[l], "b_conv": vec(b_conv), "b_gate_a": vec(b_gate_a), "b_gate_x": vec(b_gate_x),
        "lru_lambda": vec(lru_lambda), "g_q": vec(g_q), "g_kv": vec(g_kv), "g_pre_mix": vec(g_pre_mix),
        "g_post_mix": vec(g_post_mix), "g_pre_mlp": vec(g_pre_mlp), "g_post_mlp": vec(g_post_mlp),
    }


def _tile(n, pref):
    return pref if n % pref == 0 else n


def kernel(x_prompt, x_sample, c_prompt, c_sample, cache_latent, cache_krope, state_conv, state_rnn, page_table,
           w_ada, b_ada, g_pre_mix, g_post_mix, g_pre_mlp, g_post_mlp, w_in, w_conv, b_conv, w_gate_a, b_gate_a,
           w_gate_x, b_gate_x, lru_lambda, g_q, w_uq, g_kv, w_uk, w_uv, w_o, w_up, w_down):
    bp, s_len, _ = x_prompt.shape
    bd, t_dec, _ = x_sample.shape
    depth = w_in.shape[0]
    assert bp == 1 and t_dec == 1, "one prompt sequence and one new token per decode row"
    n_pages = page_table.shape[1]
    past_len = n_pages * PAGE_SIZE
    pages = next(p for p in (16, 8, 4, 2, 1) if n_pages % p == 0)

    xp = x_prompt.reshape(s_len, D_MODEL)
    xs = x_sample.reshape(bd, D_MODEL)
    c_all = jnp.concatenate([c_prompt, jnp.zeros((7, D_MODEL), F32), c_sample], axis=0)
    tabs_p = _rope_tables(jnp.arange(s_len))
    tabs_s = _rope_tables(jnp.full((bd,), past_len))
    tm_in, tm_rnn, tm_out = _tile(s_len, 256), _tile(s_len, 256), _tile(s_len, 512)
    tq, tk = _tile(s_len, 128), _tile(s_len, 512)

    outs = [[] for _ in range(8)]
    for l in range(depth):
        w = _layer_weights(l, w_in, w_conv, b_conv, w_gate_a, b_gate_a, w_gate_x, b_gate_x, lru_lambda, g_q,
                           w_uq, g_kv, w_uk, w_uv, w_o, w_up, w_down, g_pre_mix, g_post_mix, g_pre_mlp,
                           g_post_mlp)
        mod = _ada(c_all, w_ada[l].astype(BF16), b_ada[l].reshape(1, -1))
        mod_p = [mod[0:1, k * D_MODEL:(k + 1) * D_MODEL] for k in range(6)]
        mod_s = [mod[8:, k * D_MODEL:(k + 1) * D_MODEL] for k in range(6)]

        xr, g_rnn, g_att, q_lat, q_pe, lat, lat_b, kpe, kpe_b = _inproj(xp, mod_p[0], mod_p[1], w, tabs_p, tm_in)
        yg, h_last = _rglru_prompt(xr, g_rnn, w, tm_rnn)
        merged = _attn_prompt(q_lat, q_pe, lat_b, kpe_b, g_att, yg, w, tq, tk)
        xp = _outmlp(merged, xp, mod_p[2], mod_p[3], mod_p[4], mod_p[5], w, tm_out, 1024)
        outs[0].append(lat.reshape(1, s_len, KV_LORA))
        outs[1].append(kpe.reshape(1, s_len, QK_ROPE))
        outs[2].append(xr[s_len - (CONV_W - 1):].reshape(1, CONV_W - 1, D_RNN))
        outs[3].append(h_last)

        xr, g_rnn, g_att, q_lat, q_pe, lat, _, kpe, _ = _inproj(xs, mod_s[0], mod_s[1], w, tabs_s, bd)
        conv_rows = [state_conv[l][:, k, :] for k in range(CONV_W - 1)]
        yg, h_new = _rglru_sample(xr, g_rnn, conv_rows, state_rnn[l], w, at_start=(past_len == 0))
        o = _attn_sample(q_lat, q_pe, lat, kpe, cache_latent[l:l + 1], cache_krope[l:l + 1], page_table, pages)
        merged = _merge_sample(o, g_att, yg, w)
        xs = _outmlp(merged, xs, mod_s[2], mod_s[3], mod_s[4], mod_s[5], w, bd, 1024)
        outs[4].append(lat.reshape(bd, 1, KV_LORA))
        outs[5].append(kpe.reshape(bd, 1, QK_ROPE))
        outs[6].append(jnp.stack(conv_rows[1:] + [xr], axis=1))
        outs[7].append(h_new)

    return (xp.reshape(1, s_len, D_MODEL), xs.reshape(bd, 1, D_MODEL)) + tuple(jnp.stack(o) for o in outs)
```

```python
import functools

import jax
import jax.numpy as jnp
from jax import lax
from jax.experimental import pallas as pl
from jax.experimental.pallas import tpu as pltpu

D_MODEL = 1024
D_RNN = 1024
RNN_BLOCKS = 16
RNN_BW = D_RNN // RNN_BLOCKS
CONV_W = 4
LRU_C = 8.0
N_HEADS = 8
QK_NOPE = 128
QK_ROPE = 64
V_DIM = D_MODEL // N_HEADS
Q_LORA = 384
KV_LORA = 256
ROPE_THETA = 10000.0
SM_SCALE = (QK_NOPE + QK_ROPE) ** -0.5
Q_SCALE = SM_SCALE * 1.4426950408889634
D_FF = 4 * D_MODEL
EPS = 1e-6
NEG = -1e30
PAGE_SIZE = 128

LANES = 128
MXU_DIM = 256
VMEM_LIMIT = 56 * 1024 * 1024

Z_XR = 0
Z_CQ = Z_XR + D_RNN
Z_CKV = Z_CQ + Q_LORA
Z_GR = Z_CKV + KV_LORA
Z_GA = Z_GR + D_MODEL
Z_KPE = Z_GA + D_MODEL
Z_END = Z_KPE + LANES

BF16 = jnp.bfloat16
F32 = jnp.float32


def _rms(x, g):
    return x * lax.rsqrt(jnp.mean(x * x, axis=-1, keepdims=True) + EPS) * g


def _dot(a, b):
    return jnp.dot(a, b, preferred_element_type=F32)


def _dot_nt(a, b):
    return lax.dot_general(a, b, (((1,), (1,)), ((), ())), preferred_element_type=F32)


def _const_spec(shape):
    zeros = (0,) * len(shape)
    return pl.BlockSpec(shape, lambda *_: zeros, pipeline_mode=pl.Buffered(1))


def _params(n_axes=1):
    return pltpu.CompilerParams(dimension_semantics=("arbitrary",) * n_axes, vmem_limit_bytes=VMEM_LIMIT)


def _ada_kernel(c_ref, w_ref, b_ref, o_ref):
    c = c_ref[...]
    o_ref[...] = _dot((c * jax.nn.sigmoid(c)).astype(BF16), w_ref[...]) + b_ref[...]


def _ada(c_all, w_ada, b_ada):
    m = c_all.shape[0]
    tn = D_MODEL
    return pl.pallas_call(
        _ada_kernel,
        grid=(6 * D_MODEL // tn,),
        in_specs=[pl.BlockSpec((m, D_MODEL), lambda j: (0, 0)),
                  pl.BlockSpec((D_MODEL, tn), lambda j: (0, j)),
                  pl.BlockSpec((1, tn), lambda j: (0, j))],
        out_specs=pl.BlockSpec((m, tn), lambda j: (0, j)),
        out_shape=jax.ShapeDtypeStruct((m, 6 * D_MODEL), F32),
        compiler_params=_params(),
        name="ada",
    )(c_all, w_ada, b_ada)


def _swap_halves_64(x):
    n = x.shape[-1]
    lane = lax.broadcasted_iota(jnp.int32, x.shape, x.ndim - 1)
    from_hi = pltpu.roll(x, n - QK_ROPE // 2, x.ndim - 1)
    from_lo = pltpu.roll(x, QK_ROPE // 2, x.ndim - 1)
    return jnp.where(lane % QK_ROPE < QK_ROPE // 2, from_hi, from_lo)


def _inproj_kernel(x_ref, sh_ref, sc_ref, gpre_ref, win_ref, gq_ref, wuq_ref, wuk_ref, gkv_ref,
                   cos_ref, sin_ref,
                   xr_ref, gr_ref, ga_ref, qlat_ref, qpe_ref, lat_ref, latb_ref, kpe_ref, latT_ref, kpeT_ref):
    x = x_ref[...]
    h = _rms(x, gpre_ref[...]) * (1.0 + sc_ref[...]) + sh_ref[...]
    z = _dot(h.astype(BF16), win_ref[...])
    xr_ref[...] = z[:, Z_XR:Z_CQ]
    gr_ref[...] = z[:, Z_GR:Z_GA]
    ga_ref[...] = z[:, Z_GA:Z_KPE]

    q = _dot(_rms(z[:, Z_CQ:Z_CKV], gq_ref[...]).astype(BF16), wuq_ref[...])
    q_rope = q[:, N_HEADS * QK_NOPE:]
    cos, sin = cos_ref[...], sin_ref[...]
    reps = N_HEADS * QK_ROPE // LANES
    q_rope = (q_rope * jnp.concatenate([cos] * reps, axis=1)
              + _swap_halves_64(q_rope) * jnp.concatenate([sin] * reps, axis=1)) * Q_SCALE
    for hd in range(N_HEADS):
        q_nope = q[:, hd * QK_NOPE:(hd + 1) * QK_NOPE].astype(BF16)
        qlat_ref[hd] = (_dot(q_nope, wuk_ref[hd]) * Q_SCALE).astype(BF16)
        qpe_ref[hd] = q_rope[:, hd * QK_ROPE:(hd + 1) * QK_ROPE].astype(BF16)

    lat = _rms(z[:, Z_CKV:Z_GR], gkv_ref[...])
    lat_ref[...] = lat
    latb_ref[...] = lat.astype(BF16)
    latT_ref[0] = lat.T.astype(BF16)
    k_raw = z[:, Z_KPE:Z_END]
    kpe = k_raw * cos + _swap_halves_64(k_raw) * sin
    kpe_ref[...] = kpe[:, :QK_ROPE]
    kpeT_ref[0] = kpe.T[:QK_ROPE, :].astype(BF16)


def _inproj(x, sh, sc, w, rope_tabs, tm):
    m = x.shape[0]
    per_row = sh.shape[0] != 1
    row = lambda i: (i, 0)
    mod_spec = pl.BlockSpec((tm, D_MODEL), row) if per_row else _const_spec((1, D_MODEL))
    cos, sin = rope_tabs
    in_specs = [
        pl.BlockSpec((tm, D_MODEL), row), mod_spec, mod_spec,
        _const_spec((1, D_MODEL)), _const_spec((D_MODEL, Z_END)), _const_spec((1, Q_LORA)),
        _const_spec((Q_LORA, N_HEADS * (QK_NOPE + QK_ROPE))), _const_spec((N_HEADS, QK_NOPE, KV_LORA)),
        _const_spec((1, KV_LORA)),
        pl.BlockSpec((tm, LANES), row), pl.BlockSpec((tm, LANES), row),
    ]
    out_shape = [
        jax.ShapeDtypeStruct((m, D_RNN), F32), jax.ShapeDtypeStruct((m, D_MODEL), F32),
        jax.ShapeDtypeStruct((m, D_MODEL), F32),
        jax.ShapeDtypeStruct((N_HEADS, m, KV_LORA), BF16), jax.ShapeDtypeStruct((N_HEADS, m, QK_ROPE), BF16),
        jax.ShapeDtypeStruct((m, KV_LORA), F32), jax.ShapeDtypeStruct((m, KV_LORA), BF16),
        jax.ShapeDtypeStruct((m, QK_ROPE), F32),
        jax.ShapeDtypeStruct((m // tm, KV_LORA, tm), BF16), jax.ShapeDtypeStruct((m // tm, QK_ROPE, tm), BF16),
    ]
    out_specs = [
        pl.BlockSpec((tm, D_RNN), row), pl.BlockSpec((tm, D_MODEL), row), pl.BlockSpec((tm, D_MODEL), row),
        pl.BlockSpec((N_HEADS, tm, KV_LORA), lambda i: (0, i, 0)),
        pl.BlockSpec((N_HEADS, tm, QK_ROPE), lambda i: (0, i, 0)),
        pl.BlockSpec((tm, KV_LORA), row), pl.BlockSpec((tm, KV_LORA), row),
        pl.BlockSpec((tm, QK_ROPE), row),
        pl.BlockSpec((1, KV_LORA, tm), lambda i: (i, 0, 0)), pl.BlockSpec((1, QK_ROPE, tm), lambda i: (i, 0, 0)),
    ]
    return pl.pallas_call(
        _inproj_kernel, grid=(m // tm,), in_specs=in_specs, out_specs=out_specs, out_shape=out_shape,
        compiler_params=_params(), name="inproj",
    )(x, sh, sc, w["g_pre_mix"], w["w_in"], w["g_q"], w["w_uq"], w["w_uk"], w["g_kv"], cos, sin)


def _lru_coeffs(xc, wa_ref, ba_ref, wx_ref, bx_ref, lam_ref):
    xb = xc.astype(BF16)
    r_parts, i_parts = [], []
    for j in range(D_RNN // MXU_DIM):
        blk = xb[:, j * MXU_DIM:(j + 1) * MXU_DIM]
        r_parts.append(_dot(blk, wa_ref[j]))
        i_parts.append(_dot(blk, wx_ref[j]))
    r = jax.nn.sigmoid(jnp.concatenate(r_parts, axis=1) + ba_ref[...])
    i = jax.nn.sigmoid(jnp.concatenate(i_parts, axis=1) + bx_ref[...])
    lam = lam_ref[...]
    softplus_neg_lam = jnp.maximum(-lam, 0.0) + jnp.log1p(jnp.exp(-jnp.abs(lam)))
    log_a = -LRU_C * r * softplus_neg_lam
    a = jnp.exp(log_a)
    mult = jnp.sqrt(-jnp.tanh(log_a) * (a * a + 1.0))
    return a, mult, i * xc


def _rglru_prompt_kernel(xr_ref, g_ref, wc_ref, bc_ref, wa_ref, ba_ref, wx_ref, bx_ref, lam_ref,
                         yg_ref, hlast_ref, xp_scr, h_scr, *, tm):
    step = pl.program_id(0)

    @pl.when(step == 0)
    def _():
        xp_scr[0:8, :] = jnp.zeros((8, D_RNN), F32)
        h_scr[...] = jnp.zeros((1, D_RNN), F32)

    x = xr_ref[...]
    xp_scr[8:8 + tm, :] = x
    wc = wc_ref[...]
    xc = (bc_ref[...] + x * wc[3:4] + xp_scr[7:7 + tm, :] * wc[2:3]
          + xp_scr[6:6 + tm, :] * wc[1:2] + xp_scr[5:5 + tm, :] * wc[0:1])
    xp_scr[0:8, :] = xp_scr[tm:tm + 8, :]

    a, mult, gated = _lru_coeffs(xc, wa_ref, ba_ref, wx_ref, bx_ref, lam_ref)
    rows = lax.broadcasted_iota(jnp.int32, (tm, D_RNN), 0)
    mult = jnp.where(rows + step * tm == 0, 1.0, mult)
    b = mult * gated

    d = 1
    while d < tm:
        keep = rows >= d
        b = jnp.where(keep, a * pltpu.roll(b, d, 0) + b, b)
        a = jnp.where(keep, a * pltpu.roll(a, d, 0), a)
        d *= 2
    h = a * h_scr[...] + b
    h_scr[...] = h[tm - 1:tm, :]
    hlast_ref[...] = h[tm - 1:tm, :]
    yg_ref[...] = jax.nn.sigmoid(g_ref[...]) * h


def _rglru_prompt(xr, g_rnn, w, tm):
    m = xr.shape[0]
    row = lambda i: (i, 0)
    nblk = D_RNN // MXU_DIM
    in_specs = [pl.BlockSpec((tm, D_RNN), row), pl.BlockSpec((tm, D_MODEL), row),
                _const_spec((CONV_W, D_RNN)), _const_spec((1, D_RNN)),
                _const_spec((nblk, MXU_DIM, MXU_DIM)), _const_spec((1, D_RNN)),
                _const_spec((nblk, MXU_DIM, MXU_DIM)), _const_spec((1, D_RNN)), _const_spec((1, D_RNN))]
    return pl.pallas_call(
        functools.partial(_rglru_prompt_kernel, tm=tm), grid=(m // tm,), in_specs=in_specs,
        out_specs=[pl.BlockSpec((tm, D_MODEL), row), pl.BlockSpec((1, D_RNN), lambda i: (0, 0))],
        out_shape=[jax.ShapeDtypeStruct((m, D_MODEL), F32), jax.ShapeDtypeStruct((1, D_RNN), F32)],
        scratch_shapes=[pltpu.VMEM((tm + 8, D_RNN), F32), pltpu.VMEM((1, D_RNN), F32)],
        compiler_params=_params(), name="rglru_prompt",
    )(xr, g_rnn, w["w_conv"], w["b_conv"], w["w_gate_a"], w["b_gate_a"], w["w_gate_x"], w["b_gate_x"],
      w["lru_lambda"])


def _rglru_sample_kernel(xr_ref, g_ref, s0_ref, s1_ref, s2_ref, h0_ref, wc_ref, bc_ref, wa_ref, ba_ref,
                         wx_ref, bx_ref, lam_ref, yg_ref, h_ref, *, at_start):
    wc = wc_ref[...]
    xc = (bc_ref[...] + xr_ref[...] * wc[3:4] + s2_ref[...] * wc[2:3] + s1_ref[...] * wc[1:2]
          + s0_ref[...] * wc[0:1])
    a, mult, gated = _lru_coeffs(xc, wa_ref, ba_ref, wx_ref, bx_ref, lam_ref)
    if at_start:
        mult = jnp.ones_like(mult)
    h = a * h0_ref[...] + mult * gated
    h_ref[...] = h
    yg_ref[...] = jax.nn.sigmoid(g_ref[...]) * h


def _rglru_sample(xr, g_rnn, conv_rows, h0, w, at_start):
    m = xr.shape[0]
    nblk = D_RNN // MXU_DIM
    full = _const_spec((m, D_RNN))
    in_specs = [full] * 6 + [_const_spec((CONV_W, D_RNN)), _const_spec((1, D_RNN)),
                             _const_spec((nblk, MXU_DIM, MXU_DIM)), _const_spec((1, D_RNN)),
                             _const_spec((nblk, MXU_DIM, MXU_DIM)), _const_spec((1, D_RNN)),
                             _const_spec((1, D_RNN))]
    return pl.pallas_call(
        functools.partial(_rglru_sample_kernel, at_start=at_start), grid=(1,), in_specs=in_specs,
        out_specs=[full, full],
        out_shape=[jax.ShapeDtypeStruct((m, D_MODEL), F32), jax.ShapeDtypeStruct((m, D_RNN), F32)],
        compiler_params=_params(), name="rglru_sample",
    )(xr, g_rnn, *conv_rows, h0, w["w_conv"], w["b_conv"], w["w_gate_a"], w["b_gate_a"], w["w_gate_x"],
      w["b_gate_x"], w["lru_lambda"])


def _value_up_and_merge(o_of_head, ga_ref, yg_ref, wuv_ref, out_ref):
    for hd in range(N_HEADS):
        cols = slice(hd * V_DIM, (hd + 1) * V_DIM)
        y = _dot(o_of_head(hd).astype(BF16), wuv_ref[hd])
        out_ref[:, cols] = (jax.nn.sigmoid(ga_ref[:, cols]) * y + yg_ref[:, cols]).astype(out_ref.dtype)


def _attn_prompt_kernel(qlat_ref, qpe_ref, latT_ref, kpeT_ref, lat_ref, ga_ref, yg_ref, wuv_ref, out_ref,
                        s0_scr, s1_scr, m_scr, l_scr, acc_scr, *, tq, tk, tc, hg):
    qi = pl.program_id(0)
    rows = N_HEADS * tq
    grp = hg * tq
    n_sub = tk // tc
    m_scr[...] = jnp.full((rows, LANES), NEG, F32)
    l_scr[...] = jnp.zeros((rows, LANES), F32)
    acc_scr[...] = jnp.zeros((rows, KV_LORA), F32)

    def scores_into(s_scr, j):
        for g in range(N_HEADS // hg):
            q_lat = qlat_ref[g * hg:(g + 1) * hg].reshape(grp, KV_LORA)
            q_pe = qpe_ref[g * hg:(g + 1) * hg].reshape(grp, QK_ROPE)
            for c in range(n_sub):
                s_scr[g * grp:(g + 1) * grp, c * tc:(c + 1) * tc] = (
                    _dot(q_lat, latT_ref[j * n_sub + c]) + _dot(q_pe, kpeT_ref[j * n_sub + c]))

    def process(s_scr, j, masked):
        start = pl.multiple_of(j * tk, tk)
        v = lat_ref[pl.ds(start, tk), :]
        for g in range(N_HEADS // hg):
            r = slice(g * grp, (g + 1) * grp)
            s = s_scr[r, :]
            if masked:
                qpos = qi * tq + lax.broadcasted_iota(jnp.int32, (grp, tk), 0) % tq
                kpos = start + lax.broadcasted_iota(jnp.int32, (grp, tk), 1)
                s = jnp.where(kpos <= qpos, s, NEG)
            m_prev = m_scr[r, :]
            m_new = jnp.maximum(m_prev, jnp.max(s, axis=1, keepdims=True))
            alpha = jnp.exp2(m_prev - m_new)
            p = jnp.exp2(s - jnp.concatenate([m_new] * (tk // LANES), axis=1))
            lane_sums = p[:, :LANES]
            for c in range(1, tk // LANES):
                lane_sums = lane_sums + p[:, c * LANES:(c + 1) * LANES]
            l_scr[r, :] = alpha * l_scr[r, :] + lane_sums
            acc_scr[r, :] = (jnp.concatenate([alpha] * (KV_LORA // LANES), axis=1) * acc_scr[r, :]
                             + _dot(p.astype(BF16), v))
            m_scr[r, :] = m_new

    n_full = (qi * tq) // tk
    scores_into(s0_scr, 0)

    def body(t, carry):
        scores_into(s1_scr, 2 * t + 1)
        process(s0_scr, 2 * t, masked=False)
        scores_into(s0_scr, 2 * t + 2)
        process(s1_scr, 2 * t + 1, masked=False)
        return carry

    lax.fori_loop(0, n_full // 2, body, 0)

    @pl.when(n_full % 2 == 0)
    def _():
        process(s0_scr, n_full, masked=True)

    @pl.when(n_full % 2 == 1)
    def _():
        scores_into(s1_scr, n_full)
        process(s0_scr, n_full - 1, masked=False)
        process(s1_scr, n_full, masked=True)

    acc_scr[...] = acc_scr[...] / jnp.sum(l_scr[...], axis=1, keepdims=True)
    _value_up_and_merge(lambda hd: acc_scr[hd * tq:(hd + 1) * tq, :], ga_ref, yg_ref, wuv_ref, out_ref)


def _attn_prompt(q_lat, q_pe, lat_t, kpe_t, lat_b, g_att, yg, w, tq, tk, hg):
    s_len = lat_b.shape[0]
    n_col, _, tc = lat_t.shape
    row = lambda i: (i, 0)
    in_specs = [pl.BlockSpec((N_HEADS, tq, KV_LORA), lambda i: (0, i, 0)),
                pl.BlockSpec((N_HEADS, tq, QK_ROPE), lambda i: (0, i, 0)),
                _const_spec((n_col, KV_LORA, tc)), _const_spec((n_col, QK_ROPE, tc)),
                _const_spec((s_len, KV_LORA)),
                pl.BlockSpec((tq, D_MODEL), row), pl.BlockSpec((tq, D_MODEL), row),
                _const_spec((N_HEADS, KV_LORA, V_DIM))]
    rows = N_HEADS * tq
    return pl.pallas_call(
        functools.partial(_attn_prompt_kernel, tq=tq, tk=tk, tc=tc, hg=hg), grid=(s_len // tq,),
        in_specs=in_specs, out_specs=pl.BlockSpec((tq, D_MODEL), row),
        out_shape=jax.ShapeDtypeStruct((s_len, D_MODEL), BF16),
        scratch_shapes=[pltpu.VMEM((rows, tk), F32), pltpu.VMEM((rows, tk), F32),
                        pltpu.VMEM((rows, LANES), F32), pltpu.VMEM((rows, LANES), F32),
                        pltpu.VMEM((rows, KV_LORA), F32)],
        compiler_params=_params(), name="attn_prompt",
    )(q_lat, q_pe, lat_t, kpe_t, lat_b, g_att, yg, w["w_uv"])


def _attn_sample_kernel(pt_ref, qlat_ref, qpe_ref, latn_ref, kpen_ref, *refs, pages, streams):
    del pt_ref
    lat_refs, kpt_refs = refs[:pages], refs[pages:2 * pages]
    o_ref, m_scr, l_scr, acc_scr = refs[2 * pages:]
    c = pl.program_id(1)
    q_lat = qlat_ref[:, 0, :]
    q_pe = qpe_ref[:, 0, :]
    per = pages // streams

    @pl.when(c == 0)
    def _():
        m_scr[...] = jnp.full((streams, N_HEADS, 1), NEG, F32)
        l_scr[...] = jnp.zeros((streams, N_HEADS, 1), F32)
        acc_scr[...] = jnp.zeros((streams, N_HEADS, KV_LORA), F32)

    kls = [r[...].astype(BF16) for r in lat_refs]
    s_parts = [_dot_nt(q_lat, kl) + _dot(q_pe, kr[...].astype(BF16)) for kl, kr in zip(kls, kpt_refs)]
    probs, alphas = [], []
    for st in range(streams):
        s = jnp.concatenate(s_parts[st * per:(st + 1) * per], axis=1)
        m_prev = m_scr[st]
        m_new = jnp.maximum(m_prev, jnp.max(s, axis=1, keepdims=True))
        alpha = jnp.exp2(m_prev - m_new)
        p = jnp.exp2(s - m_new)
        l_scr[st] = alpha * l_scr[st] + jnp.sum(p, axis=1, keepdims=True)
        m_scr[st] = m_new
        probs.append(p.astype(BF16))
        alphas.append(alpha)
    for st in range(streams):
        pv = _dot(probs[st][:, :PAGE_SIZE], kls[st * per])
        for i in range(1, per):
            pv += _dot(probs[st][:, i * PAGE_SIZE:(i + 1) * PAGE_SIZE], kls[st * per + i])
        acc_scr[st] = alphas[st] * acc_scr[st] + pv

    @pl.when(c == pl.num_programs(1) - 1)
    def _():
        lat_n = latn_ref[...]
        s_n = (jnp.sum(q_lat.astype(F32) * lat_n, axis=1, keepdims=True)
               + jnp.sum(q_pe.astype(F32) * kpen_ref[...], axis=1, keepdims=True))
        m_fin = s_n
        for st in range(streams):
            m_fin = jnp.maximum(m_fin, m_scr[st])
        p_n = jnp.exp2(s_n - m_fin)
        num, den = p_n * lat_n, p_n
        for st in range(streams):
            corr = jnp.exp2(m_scr[st] - m_fin)
            num = num + acc_scr[st] * corr
            den = den + l_scr[st] * corr
        o_ref[:, 0, :] = num / den


def _attn_sample(q_lat, q_pe, lat_new, kpe_new, pool_lat, pool_kpe_t, page_table, pages, streams):
    bd, n_pages = page_table.shape
    n_chunks = n_pages // pages
    per_b = lambda b, c, pt: (0, b, 0, 0)

    def page_spec(shape, i):
        return pl.BlockSpec((None, None) + shape, lambda b, c, pt: (0, pt[b, c * pages + i], 0, 0))

    in_specs = ([pl.BlockSpec((N_HEADS, None, 1, KV_LORA), per_b), pl.BlockSpec((N_HEADS, None, 1, QK_ROPE), per_b),
                 pl.BlockSpec((None, 1, KV_LORA), lambda b, c, pt: (b, 0, 0)),
                 pl.BlockSpec((None, 1, QK_ROPE), lambda b, c, pt: (b, 0, 0))]
                + [page_spec((PAGE_SIZE, KV_LORA), i) for i in range(pages)]
                + [page_spec((QK_ROPE, PAGE_SIZE), i) for i in range(pages)])
    grid_spec = pltpu.PrefetchScalarGridSpec(
        num_scalar_prefetch=1, grid=(bd, n_chunks), in_specs=in_specs,
        out_specs=pl.BlockSpec((N_HEADS, None, 1, KV_LORA), per_b),
        scratch_shapes=[pltpu.VMEM((streams, N_HEADS, 1), F32), pltpu.VMEM((streams, N_HEADS, 1), F32),
                        pltpu.VMEM((streams, N_HEADS, KV_LORA), F32)])
    o = pl.pallas_call(
        functools.partial(_attn_sample_kernel, pages=pages, streams=streams), grid_spec=grid_spec,
        out_shape=jax.ShapeDtypeStruct((N_HEADS, bd, 1, KV_LORA), F32),
        compiler_params=_params(2), name="attn_sample",
    )(page_table, q_lat.reshape(N_HEADS, bd, 1, KV_LORA), q_pe.reshape(N_HEADS, bd, 1, QK_ROPE),
      lat_new.reshape(bd, 1, KV_LORA), kpe_new.reshape(bd, 1, QK_ROPE),
      *([pool_lat] * pages), *([pool_kpe_t] * pages))
    return o.reshape(N_HEADS, bd, KV_LORA)


def _merge_sample_kernel(o_ref, ga_ref, yg_ref, wuv_ref, out_ref):
    _value_up_and_merge(lambda hd: o_ref[hd], ga_ref, yg_ref, wuv_ref, out_ref)


def _merge_sample(o, g_att, yg, w):
    m = g_att.shape[0]
    return pl.pallas_call(
        _merge_sample_kernel, grid=(1,),
        in_specs=[_const_spec((N_HEADS, m, KV_LORA)), _const_spec((m, D_MODEL)), _const_spec((m, D_MODEL)),
                  _const_spec((N_HEADS, KV_LORA, V_DIM))],
        out_specs=_const_spec((m, D_MODEL)),
        out_shape=jax.ShapeDtypeStruct((m, D_MODEL), BF16),
        compiler_params=_params(), name="merge_sample",
    )(o, g_att, yg, w["w_uv"])


def _outmlp_kernel(mg_ref, x_ref, gt1_ref, sh2_ref, sc2_ref, gt2_ref, gpm_ref, gpre_ref, gpost_ref,
                   wo_ref, wup_ref, wdn_ref, y_ref, *, ff_chunk):
    x1 = x_ref[...] + gt1_ref[...] * _rms(_dot(mg_ref[...], wo_ref[...]), gpm_ref[...])
    h2 = (_rms(x1, gpre_ref[...]) * (1.0 + sc2_ref[...]) + sh2_ref[...]).astype(BF16)
    f = None
    for j in range(D_FF // ff_chunk):
        cols = slice(j * ff_chunk, (j + 1) * ff_chunk)
        u = jnp.maximum(_dot(h2, wup_ref[:, cols]), 0.0)
        part = _dot((u * u).astype(BF16), wdn_ref[cols, :])
        f = part if f is None else f + part
    y_ref[...] = x1 + gt2_ref[...] * _rms(f, gpost_ref[...])


def _outmlp(merged, x, gt1, sh2, sc2, gt2, w, tm, ff_chunk):
    m = x.shape[0]
    per_row = gt1.shape[0] != 1
    row = lambda i: (i, 0)
    mod_spec = pl.BlockSpec((tm, D_MODEL), row) if per_row else _const_spec((1, D_MODEL))
    vec = _const_spec((1, D_MODEL))
    in_specs = [pl.BlockSpec((tm, D_MODEL), row), pl.BlockSpec((tm, D_MODEL), row),
                mod_spec, mod_spec, mod_spec, mod_spec, vec, vec, vec,
                _const_spec((D_MODEL, D_MODEL)), _const_spec((D_MODEL, D_FF)), _const_spec((D_FF, D_MODEL))]
    return pl.pallas_call(
        functools.partial(_outmlp_kernel, ff_chunk=ff_chunk), grid=(m // tm,), in_specs=in_specs,
        out_specs=pl.BlockSpec((tm, D_MODEL), row),
        out_shape=jax.ShapeDtypeStruct((m, D_MODEL), F32),
        compiler_params=_params(), name="outmlp",
    )(merged, x, gt1, sh2, sc2, gt2, w["g_post_mix"], w["g_pre_mlp"], w["g_post_mlp"],
      w["w_o"], w["w_up"], w["w_down"])


def _rope_tables(pos):
    half = QK_ROPE // 2
    inv = 1.0 / (ROPE_THETA ** (jnp.arange(half, dtype=F32) / half))
    ang = pos.astype(F32)[:, None] * inv[None, :]
    cos, sin = jnp.cos(ang), jnp.sin(ang)
    return jnp.concatenate([cos, cos, cos, cos], axis=1), jnp.concatenate([-sin, sin, -sin, sin], axis=1)


def _block_diag_tiles(wg):
    per = MXU_DIM // RNN_BW
    wg = wg.reshape(D_RNN // MXU_DIM, per, RNN_BW, RNN_BW)
    eye = jnp.eye(per, dtype=wg.dtype)
    return jnp.einsum("tpij,pq->tpiqj", wg, eye).reshape(D_RNN // MXU_DIM, MXU_DIM, MXU_DIM).astype(BF16)


def _layer_weights(l, w_in, w_conv, b_conv, w_gate_a, b_gate_a, w_gate_x, b_gate_x, lru_lambda, g_q, w_uq,
                   g_kv, w_uk, w_uv, w_o, w_up, w_down, g_pre_mix, g_post_mix, g_pre_mlp, g_post_mlp):
    wi = w_in[l]
    s_xr, s_cq, s_ckv, s_kpe, s_gr = D_RNN, D_RNN + Q_LORA, D_RNN + Q_LORA + KV_LORA, \
        D_RNN + Q_LORA + KV_LORA + QK_ROPE, D_RNN + Q_LORA + KV_LORA + QK_ROPE + D_MODEL
    w_in_r = jnp.concatenate([wi[:, :s_ckv], wi[:, s_kpe:], wi[:, s_ckv:s_kpe],
                              jnp.zeros((D_MODEL, LANES - QK_ROPE), wi.dtype)], axis=1).astype(BF16)
    del s_xr, s_cq, s_gr
    uq = w_uq[l]
    w_uq_r = jnp.concatenate([uq[:, :, :QK_NOPE].reshape(Q_LORA, N_HEADS * QK_NOPE),
                              uq[:, :, QK_NOPE:].reshape(Q_LORA, N_HEADS * QK_ROPE)], axis=1).astype(BF16)
    vec = lambda v: v[l].reshape(1, -1)
    return {
        "w_in": w_in_r, "w_uq": w_uq_r,
        "w_uk": jnp.transpose(w_uk[l], (1, 2, 0)).astype(BF16),
        "w_uv": jnp.transpose(w_uv[l], (1, 0, 2)).astype(BF16),
        "w_o": w_o[l].astype(BF16), "w_up": w_up[l].astype(BF16), "w_down": w_down[l].astype(BF16),
        "w_gate_a": _block_diag_tiles(w_gate_a[l]), "w_gate_x": _block_diag_tiles(w_gate_x[l]),
        "w_conv": w_conv[l], "b_conv": vec(b_conv), "b_gate_a": vec(b_gate_a), "b_gate_x": vec(b_gate_x),
        "lru_lambda": vec(lru_lambda), "g_q": vec(g_q), "g_kv": vec(g_kv), "g_pre_mix": vec(g_pre_mix),
        "g_post_mix": vec(g_post_mix), "g_pre_mlp": vec(g_pre_mlp), "g_post_mlp": vec(g_post_mlp),
    }


def _tile(n, pref):
    return pref if n % pref == 0 else n


def kernel(x_prompt, x_sample, c_prompt, c_sample, cache_latent, cache_krope, state_conv, state_rnn, page_table,
           w_ada, b_ada, g_pre_mix, g_post_mix, g_pre_mlp, g_post_mlp, w_in, w_conv, b_conv, w_gate_a, b_gate_a,
           w_gate_x, b_gate_x, lru_lambda, g_q, w_uq, g_kv, w_uk, w_uv, w_o, w_up, w_down):
    bp, s_len, _ = x_prompt.shape
    bd, t_dec, _ = x_sample.shape
    depth = w_in.shape[0]
    assert bp == 1 and t_dec == 1, "one prompt sequence and one new token per decode row"
    n_pages = page_table.shape[1]
    past_len = n_pages * PAGE_SIZE
    pages = next(p for p in (32, 16, 8, 4, 2, 1) if n_pages % p == 0)
    streams = min(pages, 4)

    xp = x_prompt.reshape(s_len, D_MODEL)
    xs = x_sample.reshape(bd, D_MODEL)
    c_all = jnp.concatenate([c_prompt, jnp.zeros((7, D_MODEL), F32), c_sample], axis=0)
    tabs_p = _rope_tables(jnp.arange(s_len))
    tabs_s = _rope_tables(jnp.full((bd,), past_len))
    tm_in, tm_rnn, tm_out = _tile(s_len, 256), _tile(s_len, 256), _tile(s_len, 512)
    tq, tk, hg = _tile(s_len, 128), _tile(s_len, 512), 2

    outs = [[] for _ in range(8)]
    for l in range(depth):
        w = _layer_weights(l, w_in, w_conv, b_conv, w_gate_a, b_gate_a, w_gate_x, b_gate_x, lru_lambda, g_q,
                           w_uq, g_kv, w_uk, w_uv, w_o, w_up, w_down, g_pre_mix, g_post_mix, g_pre_mlp,
                           g_post_mlp)
        mod = _ada(c_all, w_ada[l].astype(BF16), b_ada[l].reshape(1, -1))
        mod_p = [mod[0:1, k * D_MODEL:(k + 1) * D_MODEL] for k in range(6)]
        mod_s = [mod[8:, k * D_MODEL:(k + 1) * D_MODEL] for k in range(6)]

        xr, g_rnn, g_att, q_lat, q_pe, lat, lat_b, kpe, lat_t, kpe_t = _inproj(
            xp, mod_p[0], mod_p[1], w, tabs_p, tm_in)
        yg, h_last = _rglru_prompt(xr, g_rnn, w, tm_rnn)
        merged = _attn_prompt(q_lat, q_pe, lat_t, kpe_t, lat_b, g_att, yg, w, tq, tk, hg)
        xp = _outmlp(merged, xp, mod_p[2], mod_p[3], mod_p[4], mod_p[5], w, tm_out, 1024)
        outs[0].append(lat.reshape(1, s_len, KV_LORA))
        outs[1].append(kpe.reshape(1, s_len, QK_ROPE))
        outs[2].append(xr[s_len - (CONV_W - 1):].reshape(1, CONV_W - 1, D_RNN))
        outs[3].append(h_last)

        xr, g_rnn, g_att, q_lat, q_pe, lat, _, kpe, _, _ = _inproj(xs, mod_s[0], mod_s[1], w, tabs_s, bd)
        conv_rows = [state_conv[l][:, k, :] for k in range(CONV_W - 1)]
        yg, h_new = _rglru_sample(xr, g_rnn, conv_rows, state_rnn[l], w, at_start=(past_len == 0))
        pool_kpe_t = jnp.swapaxes(cache_krope[l:l + 1], 2, 3)
        o = _attn_sample(q_lat, q_pe, lat, kpe, cache_latent[l:l + 1], pool_kpe_t, page_table, pages, streams)
        merged = _merge_sample(o, g_att, yg, w)
        xs = _outmlp(merged, xs, mod_s[2], mod_s[3], mod_s[4], mod_s[5], w, bd, 1024)
        outs[4].append(lat.reshape(bd, 1, KV_LORA))
        outs[5].append(kpe.reshape(bd, 1, QK_ROPE))
        outs[6].append(jnp.stack(conv_rows[1:] + [xr], axis=1))
        outs[7].append(h_new)

    return (xp.reshape(1, s_len, D_MODEL), xs.reshape(bd, 1, D_MODEL)) + tuple(jnp.stack(o) for o in outs)
```

```python
import functools

import jax
import jax.numpy as jnp
from jax import lax
from jax.experimental import pallas as pl
from jax.experimental.pallas import tpu as pltpu

D_MODEL = 1024
D_RNN = 1024
RNN_BLOCKS = 16
RNN_BW = D_RNN // RNN_BLOCKS
CONV_W = 4
LRU_C = 8.0
N_HEADS = 8
QK_NOPE = 128
QK_ROPE = 64
V_DIM = D_MODEL // N_HEADS
Q_LORA = 384
KV_LORA = 256
ROPE_THETA = 10000.0
SM_SCALE = (QK_NOPE + QK_ROPE) ** -0.5
Q_SCALE = SM_SCALE * 1.4426950408889634
D_FF = 4 * D_MODEL
EPS = 1e-6
NEG = -1e30
PAGE_SIZE = 128

LANES = 128
SUBLANES = 8
MXU_DIM = 256
VMEM_LIMIT = 56 * 1024 * 1024

Z_XR = 0
Z_CQ = Z_XR + D_RNN
Z_CKV = Z_CQ + Q_LORA
Z_GR = Z_CKV + KV_LORA
Z_GA = Z_GR + D_MODEL
Z_KPE = Z_GA + D_MODEL
Z_END = Z_KPE + LANES

BF16 = jnp.bfloat16
F32 = jnp.float32


def _rms(x, g):
    return x * lax.rsqrt(jnp.mean(x * x, axis=-1, keepdims=True) + EPS) * g


def _dot(a, b):
    return jnp.dot(a, b, preferred_element_type=F32)


def _dot_nt(a, b):
    return lax.dot_general(a, b, (((1,), (1,)), ((), ())), preferred_element_type=F32)


def _const_spec(shape):
    zeros = (0,) * len(shape)
    return pl.BlockSpec(shape, lambda *_: zeros, pipeline_mode=pl.Buffered(1))


def _params(n_axes=1):
    return pltpu.CompilerParams(dimension_semantics=("arbitrary",) * n_axes, vmem_limit_bytes=VMEM_LIMIT)


def _ada_kernel(c_ref, w_ref, b_ref, o_ref):
    c = c_ref[...]
    o_ref[...] = _dot((c * jax.nn.sigmoid(c)).astype(BF16), w_ref[...]) + b_ref[...]


def _ada(c_all, w_ada, b_ada):
    m = c_all.shape[0]
    tn = D_MODEL
    return pl.pallas_call(
        _ada_kernel,
        grid=(6 * D_MODEL // tn,),
        in_specs=[pl.BlockSpec((m, D_MODEL), lambda j: (0, 0)),
                  pl.BlockSpec((D_MODEL, tn), lambda j: (0, j)),
                  pl.BlockSpec((1, tn), lambda j: (0, j))],
        out_specs=pl.BlockSpec((m, tn), lambda j: (0, j)),
        out_shape=jax.ShapeDtypeStruct((m, 6 * D_MODEL), F32),
        compiler_params=_params(),
        name="ada",
    )(c_all, w_ada, b_ada)


def _swap_halves_64(x):
    n = x.shape[-1]
    lane = lax.broadcasted_iota(jnp.int32, x.shape, x.ndim - 1)
    from_hi = pltpu.roll(x, n - QK_ROPE // 2, x.ndim - 1)
    from_lo = pltpu.roll(x, QK_ROPE // 2, x.ndim - 1)
    return jnp.where(lane % QK_ROPE < QK_ROPE // 2, from_hi, from_lo)


def _inproj_kernel(x_ref, sh_ref, sc_ref, gpre_ref, win_ref, gq_ref, wuq_ref, wuk_ref, gkv_ref,
                   cos_ref, sin_ref,
                   xr_ref, gr_ref, ga_ref, qlat_ref, qpe_ref, lat_ref, latb_ref, kpe_ref, latT_ref, kpeT_ref):
    x = x_ref[...]
    h = _rms(x, gpre_ref[...]) * (1.0 + sc_ref[...]) + sh_ref[...]
    z = _dot(h.astype(BF16), win_ref[...])
    xr_ref[...] = z[:, Z_XR:Z_CQ]
    gr_ref[...] = z[:, Z_GR:Z_GA]
    ga_ref[...] = z[:, Z_GA:Z_KPE]

    q = _dot(_rms(z[:, Z_CQ:Z_CKV], gq_ref[...]).astype(BF16), wuq_ref[...])
    q_rope = q[:, N_HEADS * QK_NOPE:]
    cos, sin = cos_ref[...], sin_ref[...]
    reps = N_HEADS * QK_ROPE // LANES
    q_rope = (q_rope * jnp.concatenate([cos] * reps, axis=1)
              + _swap_halves_64(q_rope) * jnp.concatenate([sin] * reps, axis=1)) * Q_SCALE
    for hd in range(N_HEADS):
        q_nope = q[:, hd * QK_NOPE:(hd + 1) * QK_NOPE].astype(BF16)
        qlat_ref[hd] = (_dot(q_nope, wuk_ref[hd]) * Q_SCALE).astype(BF16)
        qpe_ref[hd] = q_rope[:, hd * QK_ROPE:(hd + 1) * QK_ROPE].astype(BF16)

    lat = _rms(z[:, Z_CKV:Z_GR], gkv_ref[...])
    lat_ref[...] = lat
    latb_ref[...] = lat.astype(BF16)
    latT_ref[0] = lat.T.astype(BF16)
    k_raw = z[:, Z_KPE:Z_END]
    kpe = k_raw * cos + _swap_halves_64(k_raw) * sin
    kpe_ref[...] = kpe[:, :QK_ROPE]
    kpeT_ref[0] = kpe.T[:QK_ROPE, :].astype(BF16)


def _inproj(x, sh, sc, w, rope_tabs, tm):
    m = x.shape[0]
    per_row = sh.shape[0] != 1
    row = lambda i: (i, 0)
    mod_spec = pl.BlockSpec((tm, D_MODEL), row) if per_row else _const_spec((1, D_MODEL))
    cos, sin = rope_tabs
    in_specs = [
        pl.BlockSpec((tm, D_MODEL), row), mod_spec, mod_spec,
        _const_spec((1, D_MODEL)), _const_spec((D_MODEL, Z_END)), _const_spec((1, Q_LORA)),
        _const_spec((Q_LORA, N_HEADS * (QK_NOPE + QK_ROPE))), _const_spec((N_HEADS, QK_NOPE, KV_LORA)),
        _const_spec((1, KV_LORA)),
        pl.BlockSpec((tm, LANES), row), pl.BlockSpec((tm, LANES), row),
    ]
    out_shape = [
        jax.ShapeDtypeStruct((m, D_RNN), F32), jax.ShapeDtypeStruct((m, D_MODEL), F32),
        jax.ShapeDtypeStruct((m, D_MODEL), F32),
        jax.ShapeDtypeStruct((N_HEADS, m, KV_LORA), BF16), jax.ShapeDtypeStruct((N_HEADS, m, QK_ROPE), BF16),
        jax.ShapeDtypeStruct((m, KV_LORA), F32), jax.ShapeDtypeStruct((m, KV_LORA), BF16),
        jax.ShapeDtypeStruct((m, QK_ROPE), F32),
        jax.ShapeDtypeStruct((m // tm, KV_LORA, tm), BF16), jax.ShapeDtypeStruct((m // tm, QK_ROPE, tm), BF16),
    ]
    out_specs = [
        pl.BlockSpec((tm, D_RNN), row), pl.BlockSpec((tm, D_MODEL), row), pl.BlockSpec((tm, D_MODEL), row),
        pl.BlockSpec((N_HEADS, tm, KV_LORA), lambda i: (0, i, 0)),
        pl.BlockSpec((N_HEADS, tm, QK_ROPE), lambda i: (0, i, 0)),
        pl.BlockSpec((tm, KV_LORA), row), pl.BlockSpec((tm, KV_LORA), row),
        pl.BlockSpec((tm, QK_ROPE), row),
        pl.BlockSpec((1, KV_LORA, tm), lambda i: (i, 0, 0)), pl.BlockSpec((1, QK_ROPE, tm), lambda i: (i, 0, 0)),
    ]
    return pl.pallas_call(
        _inproj_kernel, grid=(m // tm,), in_specs=in_specs, out_specs=out_specs, out_shape=out_shape,
        compiler_params=_params(), name="inproj",
    )(x, sh, sc, w["g_pre_mix"], w["w_in"], w["g_q"], w["w_uq"], w["w_uk"], w["g_kv"], cos, sin)


def _lru_coeffs(xc, wa_ref, ba_ref, wx_ref, bx_ref, lam_ref):
    xb = xc.astype(BF16)
    r_parts, i_parts = [], []
    for j in range(D_RNN // MXU_DIM):
        blk = xb[:, j * MXU_DIM:(j + 1) * MXU_DIM]
        r_parts.append(_dot(blk, wa_ref[j]))
        i_parts.append(_dot(blk, wx_ref[j]))
    r = jax.nn.sigmoid(jnp.concatenate(r_parts, axis=1) + ba_ref[...])
    i = jax.nn.sigmoid(jnp.concatenate(i_parts, axis=1) + bx_ref[...])
    lam = lam_ref[...]
    softplus_neg_lam = jnp.maximum(-lam, 0.0) + jnp.log1p(jnp.exp(-jnp.abs(lam)))
    log_a = -LRU_C * r * softplus_neg_lam
    a = jnp.exp(log_a)
    mult = jnp.sqrt(-jnp.tanh(log_a) * (a * a + 1.0))
    return a, mult, i * xc


def _rglru_prompt_kernel(xr_ref, g_ref, wc_ref, bc_ref, wa_ref, ba_ref, wx_ref, bx_ref, lam_ref,
                         yg_ref, hlast_ref, xp_scr, h_scr, *, tm):
    step = pl.program_id(0)

    @pl.when(step == 0)
    def _():
        xp_scr[...] = jnp.zeros((SUBLANES, D_RNN), F32)
        h_scr[...] = jnp.zeros((1, D_RNN), F32)

    ng = tm // SUBLANES
    x3 = xr_ref[...].reshape(ng, SUBLANES, D_RNN)
    sub = lax.broadcasted_iota(jnp.int32, (ng, SUBLANES, D_RNN), 1)
    prev_group = xp_scr[...].reshape(1, SUBLANES, D_RNN)
    xp_scr[...] = x3[ng - 1]

    wc = wc_ref[...]
    xc = bc_ref[...] + x3 * wc[CONV_W - 1:CONV_W]
    for k in range(1, CONV_W):
        rot = pltpu.roll(x3, k, 1)
        rot_prev = jnp.concatenate([pltpu.roll(prev_group, k, 1), rot[:ng - 1]], axis=0)
        xc = xc + jnp.where(sub >= k, rot, rot_prev) * wc[CONV_W - 1 - k:CONV_W - k]
    xc = xc.reshape(tm, D_RNN)

    a, mult, gated = _lru_coeffs(xc, wa_ref, ba_ref, wx_ref, bx_ref, lam_ref)
    rows = lax.broadcasted_iota(jnp.int32, (tm, D_RNN), 0)
    mult = jnp.where(rows + step * tm == 0, 1.0, mult)
    a = a.reshape(ng, SUBLANES, D_RNN)
    b = (mult * gated).reshape(ng, SUBLANES, D_RNN)

    d = 1
    while d < SUBLANES:
        keep = sub >= d
        b = jnp.where(keep, a * pltpu.roll(b, d, 1) + b, b)
        a = jnp.where(keep, a * pltpu.roll(a, d, 1), a)
        d *= 2
    h_prev = h_scr[...]
    hs = []
    for g in range(ng):
        hg = a[g] * h_prev + b[g]
        hs.append(hg)
        h_prev = hg[SUBLANES - 1:SUBLANES, :]
    h_scr[...] = h_prev
    hlast_ref[...] = h_prev
    yg_ref[...] = jax.nn.sigmoid(g_ref[...]) * jnp.concatenate(hs, axis=0)


def _rglru_prompt(xr, g_rnn, w, tm):
    m = xr.shape[0]
    row = lambda i: (i, 0)
    nblk = D_RNN // MXU_DIM
    in_specs = [pl.BlockSpec((tm, D_RNN), row), pl.BlockSpec((tm, D_MODEL), row),
                _const_spec((CONV_W, D_RNN)), _const_spec((1, D_RNN)),
                _const_spec((nblk, MXU_DIM, MXU_DIM)), _const_spec((1, D_RNN)),
                _const_spec((nblk, MXU_DIM, MXU_DIM)), _const_spec((1, D_RNN)), _const_spec((1, D_RNN))]
    return pl.pallas_call(
        functools.partial(_rglru_prompt_kernel, tm=tm), grid=(m // tm,), in_specs=in_specs,
        out_specs=[pl.BlockSpec((tm, D_MODEL), row), pl.BlockSpec((1, D_RNN), lambda i: (0, 0))],
        out_shape=[jax.ShapeDtypeStruct((m, D_MODEL), F32), jax.ShapeDtypeStruct((1, D_RNN), F32)],
        scratch_shapes=[pltpu.VMEM((SUBLANES, D_RNN), F32), pltpu.VMEM((1, D_RNN), F32)],
        compiler_params=_params(), name="rglru_prompt",
    )(xr, g_rnn, w["w_conv"], w["b_conv"], w["w_gate_a"], w["b_gate_a"], w["w_gate_x"], w["b_gate_x"],
      w["lru_lambda"])


def _rglru_sample_kernel(xr_ref, g_ref, s0_ref, s1_ref, s2_ref, h0_ref, wc_ref, bc_ref, wa_ref, ba_ref,
                         wx_ref, bx_ref, lam_ref, yg_ref, h_ref, *, at_start):
    wc = wc_ref[...]
    xc = (bc_ref[...] + xr_ref[...] * wc[3:4] + s2_ref[...] * wc[2:3] + s1_ref[...] * wc[1:2]
          + s0_ref[...] * wc[0:1])
    a, mult, gated = _lru_coeffs(xc, wa_ref, ba_ref, wx_ref, bx_ref, lam_ref)
    if at_start:
        mult = jnp.ones_like(mult)
    h = a * h0_ref[...] + mult * gated
    h_ref[...] = h
    yg_ref[...] = jax.nn.sigmoid(g_ref[...]) * h


def _rglru_sample(xr, g_rnn, conv_rows, h0, w, at_start):
    m = xr.shape[0]
    nblk = D_RNN // MXU_DIM
    full = _const_spec((m, D_RNN))
    in_specs = [full] * 6 + [_const_spec((CONV_W, D_RNN)), _const_spec((1, D_RNN)),
                             _const_spec((nblk, MXU_DIM, MXU_DIM)), _const_spec((1, D_RNN)),
                             _const_spec((nblk, MXU_DIM, MXU_DIM)), _const_spec((1, D_RNN)),
                             _const_spec((1, D_RNN))]
    return pl.pallas_call(
        functools.partial(_rglru_sample_kernel, at_start=at_start), grid=(1,), in_specs=in_specs,
        out_specs=[full, full],
        out_shape=[jax.ShapeDtypeStruct((m, D_MODEL), F32), jax.ShapeDtypeStruct((m, D_RNN), F32)],
        compiler_params=_params(), name="rglru_sample",
    )(xr, g_rnn, *conv_rows, h0, w["w_conv"], w["b_conv"], w["w_gate_a"], w["b_gate_a"], w["w_gate_x"],
      w["b_gate_x"], w["lru_lambda"])


def _value_up_and_merge(o_of_head, ga_ref, yg_ref, wuv_ref, out_ref):
    for hd in range(N_HEADS):
        cols = slice(hd * V_DIM, (hd + 1) * V_DIM)
        y = _dot(o_of_head(hd).astype(BF16), wuv_ref[hd])
        out_ref[:, cols] = (jax.nn.sigmoid(ga_ref[:, cols]) * y + yg_ref[:, cols]).astype(out_ref.dtype)


def _attn_prompt_kernel(qlat_ref, qpe_ref, latT_ref, kpeT_ref, lat_ref, ga_ref, yg_ref, wuv_ref, out_ref,
                        s0_scr, s1_scr, m_scr, l_scr, acc_scr, *, tq, tk, tc, hg):
    qi = pl.program_id(0)
    rows = N_HEADS * tq
    grp = hg * tq
    n_sub = tk // tc
    m_scr[...] = jnp.full((rows, LANES), NEG, F32)
    l_scr[...] = jnp.zeros((rows, LANES), F32)
    acc_scr[...] = jnp.zeros((rows, KV_LORA), F32)

    groups = range(N_HEADS // hg)

    def scores_into(s_scr, j, g):
        q_lat = qlat_ref[g * hg:(g + 1) * hg].reshape(grp, KV_LORA)
        q_pe = qpe_ref[g * hg:(g + 1) * hg].reshape(grp, QK_ROPE)
        for c in range(n_sub):
            s_scr[g * grp:(g + 1) * grp, c * tc:(c + 1) * tc] = (
                _dot(q_lat, latT_ref[j * n_sub + c]) + _dot(q_pe, kpeT_ref[j * n_sub + c]))

    def process(s_scr, j, g, masked):
        start = pl.multiple_of(j * tk, tk)
        r = slice(g * grp, (g + 1) * grp)
        s = s_scr[r, :]
        if masked:
            qpos = qi * tq + lax.broadcasted_iota(jnp.int32, (grp, tk), 0) % tq
            kpos = start + lax.broadcasted_iota(jnp.int32, (grp, tk), 1)
            s = jnp.where(kpos <= qpos, s, NEG)
        m_prev = m_scr[r, :]
        m_new = jnp.maximum(m_prev, jnp.max(s, axis=1, keepdims=True))
        alpha = jnp.exp2(m_prev - m_new)
        p = jnp.exp2(s - jnp.concatenate([m_new] * (tk // LANES), axis=1))
        lane_sums = p[:, :LANES]
        for c in range(1, tk // LANES):
            lane_sums = lane_sums + p[:, c * LANES:(c + 1) * LANES]
        l_scr[r, :] = alpha * l_scr[r, :] + lane_sums
        acc_scr[r, :] = (jnp.concatenate([alpha] * (KV_LORA // LANES), axis=1) * acc_scr[r, :]
                         + _dot(p.astype(BF16), lat_ref[pl.ds(start, tk), :]))
        m_scr[r, :] = m_new

    def step(cur_scr, j, nxt_scr=None, masked=False):
        for g in groups:
            if nxt_scr is not None:
                scores_into(nxt_scr, j + 1, g)
            process(cur_scr, j, g, masked)

    n_full = (qi * tq) // tk
    for g in groups:
        scores_into(s0_scr, 0, g)

    def body(t, carry):
        step(s0_scr, 2 * t, s1_scr)
        step(s1_scr, 2 * t + 1, s0_scr)
        return carry

    lax.fori_loop(0, n_full // 2, body, 0)

    @pl.when(n_full % 2 == 0)
    def _():
        step(s0_scr, n_full, masked=True)

    @pl.when(n_full % 2 == 1)
    def _():
        step(s0_scr, n_full - 1, s1_scr)
        step(s1_scr, n_full, masked=True)

    acc_scr[...] = acc_scr[...] / jnp.sum(l_scr[...], axis=1, keepdims=True)
    _value_up_and_merge(lambda hd: acc_scr[hd * tq:(hd + 1) * tq, :], ga_ref, yg_ref, wuv_ref, out_ref)


def _attn_prompt(q_lat, q_pe, lat_t, kpe_t, lat_b, g_att, yg, w, tq, tk, hg):
    s_len = lat_b.shape[0]
    n_col, _, tc = lat_t.shape
    row = lambda i: (i, 0)
    in_specs = [pl.BlockSpec((N_HEADS, tq, KV_LORA), lambda i: (0, i, 0)),
                pl.BlockSpec((N_HEADS, tq, QK_ROPE), lambda i: (0, i, 0)),
                _const_spec((n_col, KV_LORA, tc)), _const_spec((n_col, QK_ROPE, tc)),
                _const_spec((s_len, KV_LORA)),
                pl.BlockSpec((tq, D_MODEL), row), pl.BlockSpec((tq, D_MODEL), row),
                _const_spec((N_HEADS, KV_LORA, V_DIM))]
    rows = N_HEADS * tq
    return pl.pallas_call(
        functools.partial(_attn_prompt_kernel, tq=tq, tk=tk, tc=tc, hg=hg), grid=(s_len // tq,),
        in_specs=in_specs, out_specs=pl.BlockSpec((tq, D_MODEL), row),
        out_shape=jax.ShapeDtypeStruct((s_len, D_MODEL), BF16),
        scratch_shapes=[pltpu.VMEM((rows, tk), F32), pltpu.VMEM((rows, tk), F32),
                        pltpu.VMEM((rows, LANES), F32), pltpu.VMEM((rows, LANES), F32),
                        pltpu.VMEM((rows, KV_LORA), F32)],
        compiler_params=_params(), name="attn_prompt",
    )(q_lat, q_pe, lat_t, kpe_t, lat_b, g_att, yg, w["w_uv"])


def _attn_sample_kernel(pt_ref, qlat_ref, qpe_ref, latn_ref, kpen_ref, lat_hbm, kpt_hbm, o_ref,
                        lat_buf, kpt_buf, lat_sem, kpt_sem, m_scr, l_scr, acc_scr, *, pages, streams):
    b, c = pl.program_id(0), pl.program_id(1)
    n_c = pl.num_programs(1)
    last_step = pl.num_programs(0) * n_c - 1
    step = b * n_c + c
    q_lat = qlat_ref[:, 0, :]
    q_pe = qpe_ref[:, 0, :]
    per = pages // streams

    def page_copies(row, chunk, slot, i):
        page = pt_ref[row, chunk * pages + i]
        return (pltpu.make_async_copy(lat_hbm.at[0, page], lat_buf.at[slot, i], lat_sem.at[slot, i]),
                pltpu.make_async_copy(kpt_hbm.at[0, page], kpt_buf.at[slot, i], kpt_sem.at[slot, i]))

    def start_chunk(row, chunk, slot, i):
        for cp in page_copies(row, chunk, slot, i):
            cp.start()

    def wait_chunk(row, chunk, slot):
        for i in range(pages):
            for cp in page_copies(row, chunk, slot, i):
                cp.wait()

    def consume(slot, prefetch):
        kls, s_parts = [], []
        for i in range(pages):
            prefetch(i)
            kl = lat_buf[slot, i].astype(BF16)
            kls.append(kl)
            s_parts.append(_dot_nt(q_lat, kl) + _dot(q_pe, kpt_buf[slot, i].astype(BF16)))
        probs, alphas = [], []
        for st in range(streams):
            s = jnp.concatenate(s_parts[st * per:(st + 1) * per], axis=1)
            m_prev = m_scr[st]
            m_new = jnp.maximum(m_prev, jnp.max(s, axis=1, keepdims=True))
            alpha = jnp.exp2(m_prev - m_new)
            p = jnp.exp2(s - m_new)
            l_scr[st] = alpha * l_scr[st] + jnp.sum(p, axis=1, keepdims=True)
            m_scr[st] = m_new
            probs.append(p.astype(BF16))
            alphas.append(alpha)
        for st in range(streams):
            pv = _dot(probs[st][:, :PAGE_SIZE], kls[st * per])
            for i in range(1, per):
                pv += _dot(probs[st][:, i * PAGE_SIZE:(i + 1) * PAGE_SIZE], kls[st * per + i])
            acc_scr[st] = alphas[st] * acc_scr[st] + pv

    @pl.when(c == 0)
    def _():
        m_scr[...] = jnp.full((streams, N_HEADS, 1), NEG, F32)
        l_scr[...] = jnp.zeros((streams, N_HEADS, 1), F32)
        acc_scr[...] = jnp.zeros((streams, N_HEADS, KV_LORA), F32)

    @pl.when(step == 0)
    def _():
        for i in range(pages):
            start_chunk(0, 0, 0, i)

    nxt = jnp.minimum(step + 1, last_step)
    nxt_row, nxt_chunk = nxt // n_c, 2 * (nxt % n_c)

    wait_chunk(b, 2 * c, 0)
    consume(0, lambda i: start_chunk(b, 2 * c + 1, 1, i))
    wait_chunk(b, 2 * c + 1, 1)
    consume(1, lambda i: start_chunk(nxt_row, nxt_chunk, 0, i))

    @pl.when(step == last_step)
    def _():
        wait_chunk(nxt_row, nxt_chunk, 0)

    @pl.when(c == n_c - 1)
    def _():
        lat_n = latn_ref[...]
        s_n = (jnp.sum(q_lat.astype(F32) * lat_n, axis=1, keepdims=True)
               + jnp.sum(q_pe.astype(F32) * kpen_ref[...], axis=1, keepdims=True))
        m_fin = s_n
        for st in range(streams):
            m_fin = jnp.maximum(m_fin, m_scr[st])
        p_n = jnp.exp2(s_n - m_fin)
        num, den = p_n * lat_n, p_n
        for st in range(streams):
            corr = jnp.exp2(m_scr[st] - m_fin)
            num = num + acc_scr[st] * corr
            den = den + l_scr[st] * corr
        o_ref[:, 0, :] = num / den


def _attn_sample(q_lat, q_pe, lat_new, kpe_new, pool_lat, pool_kpe_t, page_table, pages, streams):
    bd, n_pages = page_table.shape
    assert n_pages % (2 * pages) == 0, "each grid step consumes two chunks of pages"
    per_b = lambda b, c, pt: (0, b, 0, 0)
    in_specs = [pl.BlockSpec((N_HEADS, None, 1, KV_LORA), per_b), pl.BlockSpec((N_HEADS, None, 1, QK_ROPE), per_b),
                pl.BlockSpec((None, 1, KV_LORA), lambda b, c, pt: (b, 0, 0)),
                pl.BlockSpec((None, 1, QK_ROPE), lambda b, c, pt: (b, 0, 0)),
                pl.BlockSpec(memory_space=pl.ANY), pl.BlockSpec(memory_space=pl.ANY)]
    grid_spec = pltpu.PrefetchScalarGridSpec(
        num_scalar_prefetch=1, grid=(bd, n_pages // (2 * pages)), in_specs=in_specs,
        out_specs=pl.BlockSpec((N_HEADS, None, 1, KV_LORA), per_b),
        scratch_shapes=[pltpu.VMEM((2, pages, PAGE_SIZE, KV_LORA), F32),
                        pltpu.VMEM((2, pages, QK_ROPE, PAGE_SIZE), F32),
                        pltpu.SemaphoreType.DMA((2, pages)), pltpu.SemaphoreType.DMA((2, pages)),
                        pltpu.VMEM((streams, N_HEADS, 1), F32), pltpu.VMEM((streams, N_HEADS, 1), F32),
                        pltpu.VMEM((streams, N_HEADS, KV_LORA), F32)])
    o = pl.pallas_call(
        functools.partial(_attn_sample_kernel, pages=pages, streams=streams), grid_spec=grid_spec,
        out_shape=jax.ShapeDtypeStruct((N_HEADS, bd, 1, KV_LORA), F32),
        compiler_params=_params(2), name="attn_sample",
    )(page_table, q_lat.reshape(N_HEADS, bd, 1, KV_LORA), q_pe.reshape(N_HEADS, bd, 1, QK_ROPE),
      lat_new.reshape(bd, 1, KV_LORA), kpe_new.reshape(bd, 1, QK_ROPE), pool_lat, pool_kpe_t)
    return o.reshape(N_HEADS, bd, KV_LORA)


def _merge_sample_kernel(o_ref, ga_ref, yg_ref, wuv_ref, out_ref):
    _value_up_and_merge(lambda hd: o_ref[hd], ga_ref, yg_ref, wuv_ref, out_ref)


def _merge_sample(o, g_att, yg, w):
    m = g_att.shape[0]
    return pl.pallas_call(
        _merge_sample_kernel, grid=(1,),
        in_specs=[_const_spec((N_HEADS, m, KV_LORA)), _const_spec((m, D_MODEL)), _const_spec((m, D_MODEL)),
                  _const_spec((N_HEADS, KV_LORA, V_DIM))],
        out_specs=_const_spec((m, D_MODEL)),
        out_shape=jax.ShapeDtypeStruct((m, D_MODEL), BF16),
        compiler_params=_params(), name="merge_sample",
    )(o, g_att, yg, w["w_uv"])


def _outmlp_kernel(mg_ref, x_ref, gt1_ref, sh2_ref, sc2_ref, gt2_ref, gpm_ref, gpre_ref, gpost_ref,
                   wo_ref, wup_ref, wdn_ref, y_ref, *, ff_chunk):
    x1 = x_ref[...] + gt1_ref[...] * _rms(_dot(mg_ref[...], wo_ref[...]), gpm_ref[...])
    h2 = (_rms(x1, gpre_ref[...]) * (1.0 + sc2_ref[...]) + sh2_ref[...]).astype(BF16)
    f = None
    for j in range(D_FF // ff_chunk):
        cols = slice(j * ff_chunk, (j + 1) * ff_chunk)
        u = jnp.maximum(_dot(h2, wup_ref[:, cols]), 0.0)
        part = _dot((u * u).astype(BF16), wdn_ref[cols, :])
        f = part if f is None else f + part
    y_ref[...] = x1 + gt2_ref[...] * _rms(f, gpost_ref[...])


def _outmlp(merged, x, gt1, sh2, sc2, gt2, w, tm, ff_chunk):
    m = x.shape[0]
    per_row = gt1.shape[0] != 1
    row = lambda i: (i, 0)
    mod_spec = pl.BlockSpec((tm, D_MODEL), row) if per_row else _const_spec((1, D_MODEL))
    vec = _const_spec((1, D_MODEL))
    in_specs = [pl.BlockSpec((tm, D_MODEL), row), pl.BlockSpec((tm, D_MODEL), row),
                mod_spec, mod_spec, mod_spec, mod_spec, vec, vec, vec,
                _const_spec((D_MODEL, D_MODEL)), _const_spec((D_MODEL, D_FF)), _const_spec((D_FF, D_MODEL))]
    return pl.pallas_call(
        functools.partial(_outmlp_kernel, ff_chunk=ff_chunk), grid=(m // tm,), in_specs=in_specs,
        out_specs=pl.BlockSpec((tm, D_MODEL), row),
        out_shape=jax.ShapeDtypeStruct((m, D_MODEL), F32),
        compiler_params=_params(), name="outmlp",
    )(merged, x, gt1, sh2, sc2, gt2, w["g_post_mix"], w["g_pre_mlp"], w["g_post_mlp"],
      w["w_o"], w["w_up"], w["w_down"])


def _rope_tables(pos):
    half = QK_ROPE // 2
    inv = 1.0 / (ROPE_THETA ** (jnp.arange(half, dtype=F32) / half))
    ang = pos.astype(F32)[:, None] * inv[None, :]
    cos, sin = jnp.cos(ang), jnp.sin(ang)
    return jnp.concatenate([cos, cos, cos, cos], axis=1), jnp.concatenate([-sin, sin, -sin, sin], axis=1)


def _block_diag_tiles(wg):
    per = MXU_DIM // RNN_BW
    wg = wg.reshape(D_RNN // MXU_DIM, per, RNN_BW, RNN_BW)
    eye = jnp.eye(per, dtype=wg.dtype)
    return jnp.einsum("tpij,pq->tpiqj", wg, eye).reshape(D_RNN // MXU_DIM, MXU_DIM, MXU_DIM).astype(BF16)


def _layer_weights(l, w_in, w_conv, b_conv, w_gate_a, b_gate_a, w_gate_x, b_gate_x, lru_lambda, g_q, w_uq,
                   g_kv, w_uk, w_uv, w_o, w_up, w_down, g_pre_mix, g_post_mix, g_pre_mlp, g_post_mlp):
    wi = w_in[l]
    s_xr, s_cq, s_ckv, s_kpe, s_gr = D_RNN, D_RNN + Q_LORA, D_RNN + Q_LORA + KV_LORA, \
        D_RNN + Q_LORA + KV_LORA + QK_ROPE, D_RNN + Q_LORA + KV_LORA + QK_ROPE + D_MODEL
    w_in_r = jnp.concatenate([wi[:, :s_ckv], wi[:, s_kpe:], wi[:, s_ckv:s_kpe],
                              jnp.zeros((D_MODEL, LANES - QK_ROPE), wi.dtype)], axis=1).astype(BF16)
    del s_xr, s_cq, s_gr
    uq = w_uq[l]
    w_uq_r = jnp.concatenate([uq[:, :, :QK_NOPE].reshape(Q_LORA, N_HEADS * QK_NOPE),
                              uq[:, :, QK_NOPE:].reshape(Q_LORA, N_HEADS * QK_ROPE)], axis=1).astype(BF16)
    vec = lambda v: v[l].reshape(1, -1)
    return {
        "w_in": w_in_r, "w_uq": w_uq_r,
        "w_uk": jnp.transpose(w_uk[l], (1, 2, 0)).astype(BF16),
        "w_uv": jnp.transpose(w_uv[l], (1, 0, 2)).astype(BF16),
        "w_o": w_o[l].astype(BF16), "w_up": w_up[l].astype(BF16), "w_down": w_down[l].astype(BF16),
        "w_gate_a": _block_diag_tiles(w_gate_a[l]), "w_gate_x": _block_diag_tiles(w_gate_x[l]),
        "w_conv": w_conv[l], "b_conv": vec(b_conv), "b_gate_a": vec(b_gate_a), "b_gate_x": vec(b_gate_x),
        "lru_lambda": vec(lru_lambda), "g_q": vec(g_q), "g_kv": vec(g_kv), "g_pre_mix": vec(g_pre_mix),
        "g_post_mix": vec(g_post_mix), "g_pre_mlp": vec(g_pre_mlp), "g_post_mlp": vec(g_post_mlp),
    }


def _tile(n, pref):
    return pref if n % pref == 0 else n


def kernel(x_prompt, x_sample, c_prompt, c_sample, cache_latent, cache_krope, state_conv, state_rnn, page_table,
           w_ada, b_ada, g_pre_mix, g_post_mix, g_pre_mlp, g_post_mlp, w_in, w_conv, b_conv, w_gate_a, b_gate_a,
           w_gate_x, b_gate_x, lru_lambda, g_q, w_uq, g_kv, w_uk, w_uv, w_o, w_up, w_down):
    bp, s_len, _ = x_prompt.shape
    bd, t_dec, _ = x_sample.shape
    depth = w_in.shape[0]
    assert bp == 1 and t_dec == 1, "one prompt sequence and one new token per decode row"
    n_pages = page_table.shape[1]
    past_len = n_pages * PAGE_SIZE
    pages = next(p for p in (16, 8, 4, 2, 1) if n_pages % (2 * p) == 0)
    streams = min(pages, 4)

    xp = x_prompt.reshape(s_len, D_MODEL)
    xs = x_sample.reshape(bd, D_MODEL)
    c_all = jnp.concatenate([c_prompt, jnp.zeros((7, D_MODEL), F32), c_sample], axis=0)
    tabs_p = _rope_tables(jnp.arange(s_len))
    tabs_s = _rope_tables(jnp.full((bd,), past_len))
    tm_in, tm_rnn, tm_out = _tile(s_len, 256), _tile(s_len, 256), _tile(s_len, 512)
    tq, tk, hg = _tile(s_len, 128), _tile(s_len, 512), 2

    outs = [[] for _ in range(8)]
    for l in range(depth):
        w = _layer_weights(l, w_in, w_conv, b_conv, w_gate_a, b_gate_a, w_gate_x, b_gate_x, lru_lambda, g_q,
                           w_uq, g_kv, w_uk, w_uv, w_o, w_up, w_down, g_pre_mix, g_post_mix, g_pre_mlp,
                           g_post_mlp)
        mod = _ada(c_all, w_ada[l].astype(BF16), b_ada[l].reshape(1, -1))
        mod_p = [mod[0:1, k * D_MODEL:(k + 1) * D_MODEL] for k in range(6)]
        mod_s = [mod[8:, k * D_MODEL:(k + 1) * D_MODEL] for k in range(6)]

        xr, g_rnn, g_att, q_lat, q_pe, lat, lat_b, kpe, lat_t, kpe_t = _inproj(
            xp, mod_p[0], mod_p[1], w, tabs_p, tm_in)
        yg, h_last = _rglru_prompt(xr, g_rnn, w, tm_rnn)
        merged = _attn_prompt(q_lat, q_pe, lat_t, kpe_t, lat_b, g_att, yg, w, tq, tk, hg)
        xp = _outmlp(merged, xp, mod_p[2], mod_p[3], mod_p[4], mod_p[5], w, tm_out, 1024)
        outs[0].append(lat.reshape(1, s_len, KV_LORA))
        outs[1].append(kpe.reshape(1, s_len, QK_ROPE))
        outs[2].append(xr[s_len - (CONV_W - 1):].reshape(1, CONV_W - 1, D_RNN))
        outs[3].append(h_last)

        xr, g_rnn, g_att, q_lat, q_pe, lat, _, kpe, _, _ = _inproj(xs, mod_s[0], mod_s[1], w, tabs_s, bd)
        conv_rows = [state_conv[l][:, k, :] for k in range(CONV_W - 1)]
        yg, h_new = _rglru_sample(xr, g_rnn, conv_rows, state_rnn[l], w, at_start=(past_len == 0))
        pool_kpe_t = jnp.swapaxes(cache_krope[l:l + 1], 2, 3)
        o = _attn_sample(q_lat, q_pe, lat, kpe, cache_latent[l:l + 1], pool_kpe_t, page_table, pages, streams)
        merged = _merge_sample(o, g_att, yg, w)
        xs = _outmlp(merged, xs, mod_s[2], mod_s[3], mod_s[4], mod_s[5], w, bd, 1024)
        outs[4].append(lat.reshape(bd, 1, KV_LORA))
        outs[5].append(kpe.reshape(bd, 1, QK_ROPE))
        outs[6].append(jnp.stack(conv_rows[1:] + [xr], axis=1))
        outs[7].append(h_new)

    return (xp.reshape(1, s_len, D_MODEL), xs.reshape(bd, 1, D_MODEL)) + tuple(jnp.stack(o) for o in outs)
```

```python
import functools

import jax
import jax.numpy as jnp
from jax import lax
from jax.experimental import pallas as pl
from jax.experimental.pallas import tpu as pltpu

D_MODEL = 1024
D_RNN = 1024
RNN_BLOCKS = 16
RNN_BW = D_RNN // RNN_BLOCKS
CONV_W = 4
LRU_C = 8.0
N_HEADS = 8
QK_NOPE = 128
QK_ROPE = 64
V_DIM = D_MODEL // N_HEADS
Q_LORA = 384
KV_LORA = 256
ROPE_THETA = 10000.0
SM_SCALE = (QK_NOPE + QK_ROPE) ** -0.5
Q_SCALE = SM_SCALE * 1.4426950408889634
D_FF = 4 * D_MODEL
EPS = 1e-6
NEG = -1e30
PAGE_SIZE = 128

LANES = 128
SUBLANES = 8
DECODE_SLOTS = 4
DECODE_AHEAD = DECODE_SLOTS - 1
MXU_DIM = 256
VMEM_LIMIT = 56 * 1024 * 1024

Z_XR = 0
Z_CQ = Z_XR + D_RNN
Z_CKV = Z_CQ + Q_LORA
Z_GR = Z_CKV + KV_LORA
Z_GA = Z_GR + D_MODEL
Z_KPE = Z_GA + D_MODEL
Z_END = Z_KPE + LANES

BF16 = jnp.bfloat16
F32 = jnp.float32


def _rms(x, g):
    return x * lax.rsqrt(jnp.mean(x * x, axis=-1, keepdims=True) + EPS) * g


def _dot(a, b):
    return jnp.dot(a, b, preferred_element_type=F32)


def _dot_nt(a, b):
    return lax.dot_general(a, b, (((1,), (1,)), ((), ())), preferred_element_type=F32)


def _const_spec(shape):
    zeros = (0,) * len(shape)
    return pl.BlockSpec(shape, lambda *_: zeros, pipeline_mode=pl.Buffered(1))


def _params(n_axes=1):
    return pltpu.CompilerParams(dimension_semantics=("arbitrary",) * n_axes, vmem_limit_bytes=VMEM_LIMIT)


def _ada_kernel(c_ref, w_ref, b_ref, o_ref):
    c = c_ref[...]
    o_ref[...] = _dot((c * jax.nn.sigmoid(c)).astype(BF16), w_ref[...]) + b_ref[...]


def _ada(c_all, w_ada, b_ada):
    m = c_all.shape[0]
    tn = D_MODEL
    return pl.pallas_call(
        _ada_kernel,
        grid=(6 * D_MODEL // tn,),
        in_specs=[pl.BlockSpec((m, D_MODEL), lambda j: (0, 0)),
                  pl.BlockSpec((D_MODEL, tn), lambda j: (0, j)),
                  pl.BlockSpec((1, tn), lambda j: (0, j))],
        out_specs=pl.BlockSpec((m, tn), lambda j: (0, j)),
        out_shape=jax.ShapeDtypeStruct((m, 6 * D_MODEL), F32),
        compiler_params=_params(),
        name="ada",
    )(c_all, w_ada, b_ada)


def _swap_halves_64(x):
    n = x.shape[-1]
    lane = lax.broadcasted_iota(jnp.int32, x.shape, x.ndim - 1)
    from_hi = pltpu.roll(x, n - QK_ROPE // 2, x.ndim - 1)
    from_lo = pltpu.roll(x, QK_ROPE // 2, x.ndim - 1)
    return jnp.where(lane % QK_ROPE < QK_ROPE // 2, from_hi, from_lo)


def _inproj_kernel(x_ref, sh_ref, sc_ref, gpre_ref, win_ref, gq_ref, wuq_ref, wuk_ref, gkv_ref,
                   cos_ref, sin_ref,
                   xr_ref, gr_ref, ga_ref, qlat_ref, qpe_ref, lat_ref, latb_ref, kpe_ref, latT_ref, kpeT_ref):
    x = x_ref[...]
    h = _rms(x, gpre_ref[...]) * (1.0 + sc_ref[...]) + sh_ref[...]
    z = _dot(h.astype(BF16), win_ref[...])
    xr_ref[...] = z[:, Z_XR:Z_CQ]
    gr_ref[...] = z[:, Z_GR:Z_GA]
    ga_ref[...] = z[:, Z_GA:Z_KPE]

    q = _dot(_rms(z[:, Z_CQ:Z_CKV], gq_ref[...]).astype(BF16), wuq_ref[...])
    q_rope = q[:, N_HEADS * QK_NOPE:]
    cos, sin = cos_ref[...], sin_ref[...]
    reps = N_HEADS * QK_ROPE // LANES
    q_rope = (q_rope * jnp.concatenate([cos] * reps, axis=1)
              + _swap_halves_64(q_rope) * jnp.concatenate([sin] * reps, axis=1)) * Q_SCALE
    for hd in range(N_HEADS):
        q_nope = q[:, hd * QK_NOPE:(hd + 1) * QK_NOPE].astype(BF16)
        qlat_ref[hd] = (_dot(q_nope, wuk_ref[hd]) * Q_SCALE).astype(BF16)
        qpe_ref[hd] = q_rope[:, hd * QK_ROPE:(hd + 1) * QK_ROPE].astype(BF16)

    lat = _rms(z[:, Z_CKV:Z_GR], gkv_ref[...])
    lat_ref[...] = lat
    latb_ref[...] = lat.astype(BF16)
    latT_ref[0] = lat.T.astype(BF16)
    k_raw = z[:, Z_KPE:Z_END]
    kpe = k_raw * cos + _swap_halves_64(k_raw) * sin
    kpe_ref[...] = kpe[:, :QK_ROPE]
    kpeT_ref[0] = kpe.T[:QK_ROPE, :].astype(BF16)


def _inproj(x, sh, sc, w, rope_tabs, tm):
    m = x.shape[0]
    per_row = sh.shape[0] != 1
    row = lambda i: (i, 0)
    mod_spec = pl.BlockSpec((tm, D_MODEL), row) if per_row else _const_spec((1, D_MODEL))
    cos, sin = rope_tabs
    in_specs = [
        pl.BlockSpec((tm, D_MODEL), row), mod_spec, mod_spec,
        _const_spec((1, D_MODEL)), _const_spec((D_MODEL, Z_END)), _const_spec((1, Q_LORA)),
        _const_spec((Q_LORA, N_HEADS * (QK_NOPE + QK_ROPE))), _const_spec((N_HEADS, QK_NOPE, KV_LORA)),
        _const_spec((1, KV_LORA)),
        pl.BlockSpec((tm, LANES), row), pl.BlockSpec((tm, LANES), row),
    ]
    out_shape = [
        jax.ShapeDtypeStruct((m, D_RNN), F32), jax.ShapeDtypeStruct((m, D_MODEL), F32),
        jax.ShapeDtypeStruct((m, D_MODEL), F32),
        jax.ShapeDtypeStruct((N_HEADS, m, KV_LORA), BF16), jax.ShapeDtypeStruct((N_HEADS, m, QK_ROPE), BF16),
        jax.ShapeDtypeStruct((m, KV_LORA), F32), jax.ShapeDtypeStruct((m, KV_LORA), BF16),
        jax.ShapeDtypeStruct((m, QK_ROPE), F32),
        jax.ShapeDtypeStruct((m // tm, KV_LORA, tm), BF16), jax.ShapeDtypeStruct((m // tm, QK_ROPE, tm), BF16),
    ]
    out_specs = [
        pl.BlockSpec((tm, D_RNN), row), pl.BlockSpec((tm, D_MODEL), row), pl.BlockSpec((tm, D_MODEL), row),
        pl.BlockSpec((N_HEADS, tm, KV_LORA), lambda i: (0, i, 0)),
        pl.BlockSpec((N_HEADS, tm, QK_ROPE), lambda i: (0, i, 0)),
        pl.BlockSpec((tm, KV_LORA), row), pl.BlockSpec((tm, KV_LORA), row),
        pl.BlockSpec((tm, QK_ROPE), row),
        pl.BlockSpec((1, KV_LORA, tm), lambda i: (i, 0, 0)), pl.BlockSpec((1, QK_ROPE, tm), lambda i: (i, 0, 0)),
    ]
    return pl.pallas_call(
        _inproj_kernel, grid=(m // tm,), in_specs=in_specs, out_specs=out_specs, out_shape=out_shape,
        compiler_params=_params(), name="inproj",
    )(x, sh, sc, w["g_pre_mix"], w["w_in"], w["g_q"], w["w_uq"], w["w_uk"], w["g_kv"], cos, sin)


def _lru_coeffs(xc, wa_ref, ba_ref, wx_ref, bx_ref, lam_ref):
    xb = xc.astype(BF16)
    r_parts, i_parts = [], []
    for j in range(D_RNN // MXU_DIM):
        blk = xb[:, j * MXU_DIM:(j + 1) * MXU_DIM]
        r_parts.append(_dot(blk, wa_ref[j]))
        i_parts.append(_dot(blk, wx_ref[j]))
    r = jax.nn.sigmoid(jnp.concatenate(r_parts, axis=1) + ba_ref[...])
    i = jax.nn.sigmoid(jnp.concatenate(i_parts, axis=1) + bx_ref[...])
    lam = lam_ref[...]
    softplus_neg_lam = jnp.maximum(-lam, 0.0) + jnp.log1p(jnp.exp(-jnp.abs(lam)))
    log_a = -LRU_C * r * softplus_neg_lam
    a = jnp.exp(log_a)
    mult = jnp.sqrt(-jnp.tanh(log_a) * (a * a + 1.0))
    return a, mult, i * xc


def _rglru_prompt_kernel(xr_ref, g_ref, wc_ref, bc_ref, wa_ref, ba_ref, wx_ref, bx_ref, lam_ref,
                         yg_ref, hlast_ref, xp_scr, h_scr, *, tm):
    step = pl.program_id(0)

    @pl.when(step == 0)
    def _():
        xp_scr[...] = jnp.zeros((SUBLANES, D_RNN), F32)
        h_scr[...] = jnp.zeros((1, D_RNN), F32)

    ng = tm // SUBLANES
    x3 = xr_ref[...].reshape(ng, SUBLANES, D_RNN)
    sub = lax.broadcasted_iota(jnp.int32, (ng, SUBLANES, D_RNN), 1)
    prev_group = xp_scr[...].reshape(1, SUBLANES, D_RNN)
    xp_scr[...] = x3[ng - 1]

    wc = wc_ref[...]
    xc = bc_ref[...] + x3 * wc[CONV_W - 1:CONV_W]
    for k in range(1, CONV_W):
        rot = pltpu.roll(x3, k, 1)
        rot_prev = jnp.concatenate([pltpu.roll(prev_group, k, 1), rot[:ng - 1]], axis=0)
        xc = xc + jnp.where(sub >= k, rot, rot_prev) * wc[CONV_W - 1 - k:CONV_W - k]
    xc = xc.reshape(tm, D_RNN)

    a, mult, gated = _lru_coeffs(xc, wa_ref, ba_ref, wx_ref, bx_ref, lam_ref)
    rows = lax.broadcasted_iota(jnp.int32, (tm, D_RNN), 0)
    mult = jnp.where(rows + step * tm == 0, 1.0, mult)
    a = a.reshape(ng, SUBLANES, D_RNN)
    b = (mult * gated).reshape(ng, SUBLANES, D_RNN)

    d = 1
    while d < SUBLANES:
        keep = sub >= d
        b = jnp.where(keep, a * pltpu.roll(b, d, 1) + b, b)
        a = jnp.where(keep, a * pltpu.roll(a, d, 1), a)
        d *= 2
    h_prev = h_scr[...]
    hs = []
    for g in range(ng):
        hg = a[g] * h_prev + b[g]
        hs.append(hg)
        h_prev = hg[SUBLANES - 1:SUBLANES, :]
    h_scr[...] = h_prev
    hlast_ref[...] = h_prev
    yg_ref[...] = jax.nn.sigmoid(g_ref[...]) * jnp.concatenate(hs, axis=0)


def _rglru_prompt(xr, g_rnn, w, tm):
    m = xr.shape[0]
    row = lambda i: (i, 0)
    nblk = D_RNN // MXU_DIM
    in_specs = [pl.BlockSpec((tm, D_RNN), row), pl.BlockSpec((tm, D_MODEL), row),
                _const_spec((CONV_W, D_RNN)), _const_spec((1, D_RNN)),
                _const_spec((nblk, MXU_DIM, MXU_DIM)), _const_spec((1, D_RNN)),
                _const_spec((nblk, MXU_DIM, MXU_DIM)), _const_spec((1, D_RNN)), _const_spec((1, D_RNN))]
    return pl.pallas_call(
        functools.partial(_rglru_prompt_kernel, tm=tm), grid=(m // tm,), in_specs=in_specs,
        out_specs=[pl.BlockSpec((tm, D_MODEL), row), pl.BlockSpec((1, D_RNN), lambda i: (0, 0))],
        out_shape=[jax.ShapeDtypeStruct((m, D_MODEL), F32), jax.ShapeDtypeStruct((1, D_RNN), F32)],
        scratch_shapes=[pltpu.VMEM((SUBLANES, D_RNN), F32), pltpu.VMEM((1, D_RNN), F32)],
        compiler_params=_params(), name="rglru_prompt",
    )(xr, g_rnn, w["w_conv"], w["b_conv"], w["w_gate_a"], w["b_gate_a"], w["w_gate_x"], w["b_gate_x"],
      w["lru_lambda"])


def _rglru_sample_kernel(xr_ref, g_ref, s0_ref, s1_ref, s2_ref, h0_ref, wc_ref, bc_ref, wa_ref, ba_ref,
                         wx_ref, bx_ref, lam_ref, yg_ref, h_ref, *, at_start):
    wc = wc_ref[...]
    xc = (bc_ref[...] + xr_ref[...] * wc[3:4] + s2_ref[...] * wc[2:3] + s1_ref[...] * wc[1:2]
          + s0_ref[...] * wc[0:1])
    a, mult, gated = _lru_coeffs(xc, wa_ref, ba_ref, wx_ref, bx_ref, lam_ref)
    if at_start:
        mult = jnp.ones_like(mult)
    h = a * h0_ref[...] + mult * gated
    h_ref[...] = h
    yg_ref[...] = jax.nn.sigmoid(g_ref[...]) * h


def _rglru_sample(xr, g_rnn, conv_rows, h0, w, at_start):
    m = xr.shape[0]
    nblk = D_RNN // MXU_DIM
    full = _const_spec((m, D_RNN))
    in_specs = [full] * 6 + [_const_spec((CONV_W, D_RNN)), _const_spec((1, D_RNN)),
                             _const_spec((nblk, MXU_DIM, MXU_DIM)), _const_spec((1, D_RNN)),
                             _const_spec((nblk, MXU_DIM, MXU_DIM)), _const_spec((1, D_RNN)),
                             _const_spec((1, D_RNN))]
    return pl.pallas_call(
        functools.partial(_rglru_sample_kernel, at_start=at_start), grid=(1,), in_specs=in_specs,
        out_specs=[full, full],
        out_shape=[jax.ShapeDtypeStruct((m, D_MODEL), F32), jax.ShapeDtypeStruct((m, D_RNN), F32)],
        compiler_params=_params(), name="rglru_sample",
    )(xr, g_rnn, *conv_rows, h0, w["w_conv"], w["b_conv"], w["w_gate_a"], w["b_gate_a"], w["w_gate_x"],
      w["b_gate_x"], w["lru_lambda"])


def _value_up_and_merge(o_of_head, ga_ref, yg_ref, wuv_ref, out_ref):
    for hd in range(N_HEADS):
        cols = slice(hd * V_DIM, (hd + 1) * V_DIM)
        y = _dot(o_of_head(hd).astype(BF16), wuv_ref[hd])
        out_ref[:, cols] = (jax.nn.sigmoid(ga_ref[:, cols]) * y + yg_ref[:, cols]).astype(out_ref.dtype)


def _attn_prompt_kernel(qlat_ref, qpe_ref, latT_ref, kpeT_ref, lat_ref, ga_ref, yg_ref, wuv_ref, out_ref,
                        s0_scr, s1_scr, m_scr, l_scr, acc_scr, *, tq, tk, tc, hg):
    qi = pl.program_id(0)
    rows = N_HEADS * tq
    grp = hg * tq
    n_sub = tk // tc
    m_scr[...] = jnp.full((rows, LANES), NEG, F32)
    l_scr[...] = jnp.zeros((rows, LANES), F32)
    acc_scr[...] = jnp.zeros((rows, KV_LORA), F32)

    groups = range(N_HEADS // hg)

    def scores_into(s_scr, j, g):
        q_lat = qlat_ref[g * hg:(g + 1) * hg].reshape(grp, KV_LORA)
        q_pe = qpe_ref[g * hg:(g + 1) * hg].reshape(grp, QK_ROPE)
        for c in range(n_sub):
            s_scr[g * grp:(g + 1) * grp, c * tc:(c + 1) * tc] = (
                _dot(q_lat, latT_ref[j * n_sub + c]) + _dot(q_pe, kpeT_ref[j * n_sub + c]))

    def process(s_scr, j, g, masked):
        start = pl.multiple_of(j * tk, tk)
        r = slice(g * grp, (g + 1) * grp)
        s = s_scr[r, :]
        if masked:
            qpos = qi * tq + lax.broadcasted_iota(jnp.int32, (grp, tk), 0) % tq
            kpos = start + lax.broadcasted_iota(jnp.int32, (grp, tk), 1)
            s = jnp.where(kpos <= qpos, s, NEG)
        m_prev = m_scr[r, :]
        m_new = jnp.maximum(m_prev, jnp.max(s, axis=1, keepdims=True))
        alpha = jnp.exp2(m_prev - m_new)
        p = jnp.exp2(s - jnp.concatenate([m_new] * (tk // LANES), axis=1))
        lane_sums = p[:, :LANES]
        for c in range(1, tk // LANES):
            lane_sums = lane_sums + p[:, c * LANES:(c + 1) * LANES]
        l_scr[r, :] = alpha * l_scr[r, :] + lane_sums
        acc_scr[r, :] = (jnp.concatenate([alpha] * (KV_LORA // LANES), axis=1) * acc_scr[r, :]
                         + _dot(p.astype(BF16), lat_ref[pl.ds(start, tk), :]))
        m_scr[r, :] = m_new

    def step(cur_scr, j, nxt_scr=None, masked=False):
        for g in groups:
            if nxt_scr is not None:
                scores_into(nxt_scr, j + 1, g)
            process(cur_scr, j, g, masked)

    n_full = (qi * tq) // tk
    for g in groups:
        scores_into(s0_scr, 0, g)

    def body(t, carry):
        step(s0_scr, 2 * t, s1_scr)
        step(s1_scr, 2 * t + 1, s0_scr)
        return carry

    lax.fori_loop(0, n_full // 2, body, 0)

    @pl.when(n_full % 2 == 0)
    def _():
        step(s0_scr, n_full, masked=True)

    @pl.when(n_full % 2 == 1)
    def _():
        step(s0_scr, n_full - 1, s1_scr)
        step(s1_scr, n_full, masked=True)

    acc_scr[...] = acc_scr[...] / jnp.sum(l_scr[...], axis=1, keepdims=True)
    _value_up_and_merge(lambda hd: acc_scr[hd * tq:(hd + 1) * tq, :], ga_ref, yg_ref, wuv_ref, out_ref)


def _attn_prompt(q_lat, q_pe, lat_t, kpe_t, lat_b, g_att, yg, w, tq, tk, hg):
    s_len = lat_b.shape[0]
    n_col, _, tc = lat_t.shape
    row = lambda i: (i, 0)
    in_specs = [pl.BlockSpec((N_HEADS, tq, KV_LORA), lambda i: (0, i, 0)),
                pl.BlockSpec((N_HEADS, tq, QK_ROPE), lambda i: (0, i, 0)),
                _const_spec((n_col, KV_LORA, tc)), _const_spec((n_col, QK_ROPE, tc)),
                _const_spec((s_len, KV_LORA)),
                pl.BlockSpec((tq, D_MODEL), row), pl.BlockSpec((tq, D_MODEL), row),
                _const_spec((N_HEADS, KV_LORA, V_DIM))]
    rows = N_HEADS * tq
    return pl.pallas_call(
        functools.partial(_attn_prompt_kernel, tq=tq, tk=tk, tc=tc, hg=hg), grid=(s_len // tq,),
        in_specs=in_specs, out_specs=pl.BlockSpec((tq, D_MODEL), row),
        out_shape=jax.ShapeDtypeStruct((s_len, D_MODEL), BF16),
        scratch_shapes=[pltpu.VMEM((rows, tk), F32), pltpu.VMEM((rows, tk), F32),
                        pltpu.VMEM((rows, LANES), F32), pltpu.VMEM((rows, LANES), F32),
                        pltpu.VMEM((rows, KV_LORA), F32)],
        compiler_params=_params(), name="attn_prompt",
    )(q_lat, q_pe, lat_t, kpe_t, lat_b, g_att, yg, w["w_uv"])


def _attn_sample_kernel(pt_ref, qlat_ref, qpe_ref, latn_ref, kpen_ref, lat_hbm, kpt_hbm, o_ref,
                        lat_buf, kpt_buf, lat_sem, kpt_sem, m_scr, l_scr, acc_scr, *, pages, streams):
    b, c = pl.program_id(0), pl.program_id(1)
    n_c = pl.num_programs(1)
    last_step = pl.num_programs(0) * n_c - 1
    step = b * n_c + c
    q_lat = qlat_ref[:, 0, :]
    q_pe = qpe_ref[:, 0, :]
    per = pages // streams

    def page_copies(row, chunk, slot, i):
        page = pt_ref[row, chunk * pages + i]
        return (pltpu.make_async_copy(lat_hbm.at[0, page], lat_buf.at[slot, i], lat_sem.at[slot, i]),
                pltpu.make_async_copy(kpt_hbm.at[0, page], kpt_buf.at[slot, i], kpt_sem.at[slot, i]))

    def start_chunk(row, chunk, slot, i):
        for cp in page_copies(row, chunk, slot, i):
            cp.start()

    def wait_chunk(row, chunk, slot):
        for i in range(pages):
            for cp in page_copies(row, chunk, slot, i):
                cp.wait()

    def consume(slot, prefetch):
        kls, s_parts = [], []
        for i in range(pages):
            prefetch(i)
            kl = lat_buf[slot, i].astype(BF16)
            kls.append(kl)
            s_parts.append(_dot_nt(q_lat, kl) + _dot(q_pe, kpt_buf[slot, i].astype(BF16)))
        probs, alphas = [], []
        for st in range(streams):
            s = jnp.concatenate(s_parts[st * per:(st + 1) * per], axis=1)
            m_prev = m_scr[st]
            m_new = jnp.maximum(m_prev, jnp.max(s, axis=1, keepdims=True))
            alpha = jnp.exp2(m_prev - m_new)
            p = jnp.exp2(s - m_new)
            l_scr[st] = alpha * l_scr[st] + jnp.sum(p, axis=1, keepdims=True)
            m_scr[st] = m_new
            probs.append(p.astype(BF16))
            alphas.append(alpha)
        for st in range(streams):
            pv = _dot(probs[st][:, :PAGE_SIZE], kls[st * per])
            for i in range(1, per):
                pv += _dot(probs[st][:, i * PAGE_SIZE:(i + 1) * PAGE_SIZE], kls[st * per + i])
            acc_scr[st] = alphas[st] * acc_scr[st] + pv

    @pl.when(c == 0)
    def _():
        m_scr[...] = jnp.full((streams, N_HEADS, 1), NEG, F32)
        l_scr[...] = jnp.zeros((streams, N_HEADS, 1), F32)
        acc_scr[...] = jnp.zeros((streams, N_HEADS, KV_LORA), F32)

    cpr = n_c * DECODE_SLOTS
    last_chunk = pl.num_programs(0) * cpr - 1
    first = step * DECODE_SLOTS

    def locate(k):
        return k // cpr, k % cpr

    @pl.when(step == 0)
    def _():
        for k in range(DECODE_AHEAD):
            for i in range(pages):
                start_chunk(*locate(k), k, i)

    for j in range(DECODE_SLOTS):
        ahead = locate(jnp.minimum(first + j + DECODE_AHEAD, last_chunk))
        ahead_slot = (j + DECODE_AHEAD) % DECODE_SLOTS
        wait_chunk(*locate(first + j), j)
        consume(j, lambda i, ahead=ahead, ahead_slot=ahead_slot: start_chunk(*ahead, ahead_slot, i))

    @pl.when(step == last_step)
    def _():
        for j in range(DECODE_SLOTS - DECODE_AHEAD, DECODE_SLOTS):
            wait_chunk(*locate(last_chunk), (j + DECODE_AHEAD) % DECODE_SLOTS)

    @pl.when(c == n_c - 1)
    def _():
        lat_n = latn_ref[...]
        s_n = (jnp.sum(q_lat.astype(F32) * lat_n, axis=1, keepdims=True)
               + jnp.sum(q_pe.astype(F32) * kpen_ref[...], axis=1, keepdims=True))
        m_fin = s_n
        for st in range(streams):
            m_fin = jnp.maximum(m_fin, m_scr[st])
        p_n = jnp.exp2(s_n - m_fin)
        num, den = p_n * lat_n, p_n
        for st in range(streams):
            corr = jnp.exp2(m_scr[st] - m_fin)
            num = num + acc_scr[st] * corr
            den = den + l_scr[st] * corr
        o_ref[:, 0, :] = num / den


def _attn_sample(q_lat, q_pe, lat_new, kpe_new, pool_lat, pool_kpe_t, page_table, pages, streams):
    bd, n_pages = page_table.shape
    assert n_pages % (DECODE_SLOTS * pages) == 0, "each grid step consumes DECODE_SLOTS chunks of pages"
    per_b = lambda b, c, pt: (0, b, 0, 0)
    in_specs = [pl.BlockSpec((N_HEADS, None, 1, KV_LORA), per_b), pl.BlockSpec((N_HEADS, None, 1, QK_ROPE), per_b),
                pl.BlockSpec((None, 1, KV_LORA), lambda b, c, pt: (b, 0, 0)),
                pl.BlockSpec((None, 1, QK_ROPE), lambda b, c, pt: (b, 0, 0)),
                pl.BlockSpec(memory_space=pl.ANY), pl.BlockSpec(memory_space=pl.ANY)]
    grid_spec = pltpu.PrefetchScalarGridSpec(
        num_scalar_prefetch=1, grid=(bd, n_pages // (DECODE_SLOTS * pages)), in_specs=in_specs,
        out_specs=pl.BlockSpec((N_HEADS, None, 1, KV_LORA), per_b),
        scratch_shapes=[pltpu.VMEM((DECODE_SLOTS, pages, PAGE_SIZE, KV_LORA), F32),
                        pltpu.VMEM((DECODE_SLOTS, pages, QK_ROPE, PAGE_SIZE), F32),
                        pltpu.SemaphoreType.DMA((DECODE_SLOTS, pages)),
                        pltpu.SemaphoreType.DMA((DECODE_SLOTS, pages)),
                        pltpu.VMEM((streams, N_HEADS, 1), F32), pltpu.VMEM((streams, N_HEADS, 1), F32),
                        pltpu.VMEM((streams, N_HEADS, KV_LORA), F32)])
    o = pl.pallas_call(
        functools.partial(_attn_sample_kernel, pages=pages, streams=streams), grid_spec=grid_spec,
        out_shape=jax.ShapeDtypeStruct((N_HEADS, bd, 1, KV_LORA), F32),
        compiler_params=_params(2), name="attn_sample",
    )(page_table, q_lat.reshape(N_HEADS, bd, 1, KV_LORA), q_pe.reshape(N_HEADS, bd, 1, QK_ROPE),
      lat_new.reshape(bd, 1, KV_LORA), kpe_new.reshape(bd, 1, QK_ROPE), pool_lat, pool_kpe_t)
    return o.reshape(N_HEADS, bd, KV_LORA)


def _merge_sample_kernel(o_ref, ga_ref, yg_ref, wuv_ref, out_ref):
    _value_up_and_merge(lambda hd: o_ref[hd], ga_ref, yg_ref, wuv_ref, out_ref)


def _merge_sample(o, g_att, yg, w):
    m = g_att.shape[0]
    return pl.pallas_call(
        _merge_sample_kernel, grid=(1,),
        in_specs=[_const_spec((N_HEADS, m, KV_LORA)), _const_spec((m, D_MODEL)), _const_spec((m, D_MODEL)),
                  _const_spec((N_HEADS, KV_LORA, V_DIM))],
        out_specs=_const_spec((m, D_MODEL)),
        out_shape=jax.ShapeDtypeStruct((m, D_MODEL), BF16),
        compiler_params=_params(), name="merge_sample",
    )(o, g_att, yg, w["w_uv"])


def _outmlp_kernel(mg_ref, x_ref, gt1_ref, sh2_ref, sc2_ref, gt2_ref, gpm_ref, gpre_ref, gpost_ref,
                   wo_ref, wup_ref, wdn_ref, y_ref, *, ff_chunk):
    x1 = x_ref[...] + gt1_ref[...] * _rms(_dot(mg_ref[...], wo_ref[...]), gpm_ref[...])
    h2 = (_rms(x1, gpre_ref[...]) * (1.0 + sc2_ref[...]) + sh2_ref[...]).astype(BF16)
    f = None
    for j in range(D_FF // ff_chunk):
        cols = slice(j * ff_chunk, (j + 1) * ff_chunk)
        u = jnp.maximum(_dot(h2, wup_ref[:, cols]), 0.0)
        part = _dot((u * u).astype(BF16), wdn_ref[cols, :])
        f = part if f is None else f + part
    y_ref[...] = x1 + gt2_ref[...] * _rms(f, gpost_ref[...])


def _outmlp(merged, x, gt1, sh2, sc2, gt2, w, tm, ff_chunk):
    m = x.shape[0]
    per_row = gt1.shape[0] != 1
    row = lambda i: (i, 0)
    mod_spec = pl.BlockSpec((tm, D_MODEL), row) if per_row else _const_spec((1, D_MODEL))
    vec = _const_spec((1, D_MODEL))
    in_specs = [pl.BlockSpec((tm, D_MODEL), row), pl.BlockSpec((tm, D_MODEL), row),
                mod_spec, mod_spec, mod_spec, mod_spec, vec, vec, vec,
                _const_spec((D_MODEL, D_MODEL)), _const_spec((D_MODEL, D_FF)), _const_spec((D_FF, D_MODEL))]
    return pl.pallas_call(
        functools.partial(_outmlp_kernel, ff_chunk=ff_chunk), grid=(m // tm,), in_specs=in_specs,
        out_specs=pl.BlockSpec((tm, D_MODEL), row),
        out_shape=jax.ShapeDtypeStruct((m, D_MODEL), F32),
        compiler_params=_params(), name="outmlp",
    )(merged, x, gt1, sh2, sc2, gt2, w["g_post_mix"], w["g_pre_mlp"], w["g_post_mlp"],
      w["w_o"], w["w_up"], w["w_down"])


def _rope_tables(pos):
    half = QK_ROPE // 2
    inv = 1.0 / (ROPE_THETA ** (jnp.arange(half, dtype=F32) / half))
    ang = pos.astype(F32)[:, None] * jnp.tile(inv, 2 * LANES // QK_ROPE)[None, :]
    sign = jnp.tile(jnp.concatenate([-jnp.ones((half,), F32), jnp.ones((half,), F32)]), LANES // QK_ROPE)
    return jnp.cos(ang), jnp.sin(ang) * sign[None, :]


def _block_diag_tiles(wg):
    per = MXU_DIM // RNN_BW
    wg = wg.reshape(D_RNN // MXU_DIM, per, RNN_BW, RNN_BW)
    eye = jnp.eye(per, dtype=wg.dtype)
    return jnp.einsum("tpij,pq->tpiqj", wg, eye).reshape(D_RNN // MXU_DIM, MXU_DIM, MXU_DIM).astype(BF16)


def _layer_weights(l, w_in, w_conv, b_conv, w_gate_a, b_gate_a, w_gate_x, b_gate_x, lru_lambda, g_q, w_uq,
                   g_kv, w_uk, w_uv, w_o, w_up, w_down, g_pre_mix, g_post_mix, g_pre_mlp, g_post_mlp):
    wi = w_in[l]
    s_xr, s_cq, s_ckv, s_kpe, s_gr = D_RNN, D_RNN + Q_LORA, D_RNN + Q_LORA + KV_LORA, \
        D_RNN + Q_LORA + KV_LORA + QK_ROPE, D_RNN + Q_LORA + KV_LORA + QK_ROPE + D_MODEL
    w_in_r = jnp.concatenate([wi[:, :s_ckv], wi[:, s_kpe:], wi[:, s_ckv:s_kpe],
                              jnp.zeros((D_MODEL, LANES - QK_ROPE), wi.dtype)], axis=1).astype(BF16)
    del s_xr, s_cq, s_gr
    uq = w_uq[l]
    w_uq_r = jnp.concatenate([uq[:, :, :QK_NOPE].reshape(Q_LORA, N_HEADS * QK_NOPE),
                              uq[:, :, QK_NOPE:].reshape(Q_LORA, N_HEADS * QK_ROPE)], axis=1).astype(BF16)
    vec = lambda v: v[l].reshape(1, -1)
    return {
        "w_in": w_in_r, "w_uq": w_uq_r,
        "w_uk": jnp.transpose(w_uk[l], (1, 2, 0)).astype(BF16),
        "w_uv": jnp.transpose(w_uv[l], (1, 0, 2)).astype(BF16),
        "w_o": w_o[l].astype(BF16), "w_up": w_up[l].astype(BF16), "w_down": w_down[l].astype(BF16),
        "w_gate_a": _block_diag_tiles(w_gate_a[l]), "w_gate_x": _block_diag_tiles(w_gate_x[l]),
        "w_conv": w_conv[l], "b_conv": vec(b_conv), "b_gate_a": vec(b_gate_a), "b_gate_x": vec(b_gate_x),
        "lru_lambda": vec(lru_lambda), "g_q": vec(g_q), "g_kv": vec(g_kv), "g_pre_mix": vec(g_pre_mix),
        "g_post_mix": vec(g_post_mix), "g_pre_mlp": vec(g_pre_mlp), "g_post_mlp": vec(g_post_mlp),
    }


def _tile(n, pref):
    return pref if n % pref == 0 else n


def kernel(x_prompt, x_sample, c_prompt, c_sample, cache_latent, cache_krope, state_conv, state_rnn, page_table,
           w_ada, b_ada, g_pre_mix, g_post_mix, g_pre_mlp, g_post_mlp, w_in, w_conv, b_conv, w_gate_a, b_gate_a,
           w_gate_x, b_gate_x, lru_lambda, g_q, w_uq, g_kv, w_uk, w_uv, w_o, w_up, w_down):
    bp, s_len, _ = x_prompt.shape
    bd, t_dec, _ = x_sample.shape
    depth = w_in.shape[0]
    assert bp == 1 and t_dec == 1, "one prompt sequence and one new token per decode row"
    n_pages = page_table.shape[1]
    past_len = n_pages * PAGE_SIZE
    pages = next(p for p in (16, 8, 4, 2, 1) if n_pages % (DECODE_SLOTS * p) == 0)
    streams = min(pages, 4)

    xp = x_prompt.reshape(s_len, D_MODEL)
    xs = x_sample.reshape(bd, D_MODEL)
    c_all = jnp.concatenate([c_prompt, jnp.zeros((7, D_MODEL), F32), c_sample], axis=0)
    tabs_p = _rope_tables(jnp.arange(s_len))
    tabs_s = _rope_tables(jnp.full((bd,), past_len))
    tm_in, tm_rnn, tm_out = _tile(s_len, 256), _tile(s_len, 256), _tile(s_len, 512)
    tq, tk, hg = _tile(s_len, 128), _tile(s_len, 512), 2

    outs = [[] for _ in range(8)]
    for l in range(depth):
        w = _layer_weights(l, w_in, w_conv, b_conv, w_gate_a, b_gate_a, w_gate_x, b_gate_x, lru_lambda, g_q,
                           w_uq, g_kv, w_uk, w_uv, w_o, w_up, w_down, g_pre_mix, g_post_mix, g_pre_mlp,
                           g_post_mlp)
        mod = _ada(c_all, w_ada[l].astype(BF16), b_ada[l].reshape(1, -1))
        mod_p = [mod[0:1, k * D_MODEL:(k + 1) * D_MODEL] for k in range(6)]
        mod_s = [mod[8:, k * D_MODEL:(k + 1) * D_MODEL] for k in range(6)]

        xr, g_rnn, g_att, q_lat, q_pe, lat, lat_b, kpe, lat_t, kpe_t = _inproj(
            xp, mod_p[0], mod_p[1], w, tabs_p, tm_in)
        yg, h_last = _rglru_prompt(xr, g_rnn, w, tm_rnn)
        merged = _attn_prompt(q_lat, q_pe, lat_t, kpe_t, lat_b, g_att, yg, w, tq, tk, hg)
        xp = _outmlp(merged, xp, mod_p[2], mod_p[3], mod_p[4], mod_p[5], w, tm_out, 1024)
        outs[0].append(lat.reshape(1, s_len, KV_LORA))
        outs[1].append(kpe.reshape(1, s_len, QK_ROPE))
        outs[2].append(xr[s_len - (CONV_W - 1):].reshape(1, CONV_W - 1, D_RNN))
        outs[3].append(h_last)

        xr, g_rnn, g_att, q_lat, q_pe, lat, _, kpe, _, _ = _inproj(xs, mod_s[0], mod_s[1], w, tabs_s, bd)
        conv_rows = [state_conv[l][:, k, :] for k in range(CONV_W - 1)]
        yg, h_new = _rglru_sample(xr, g_rnn, conv_rows, state_rnn[l], w, at_start=(past_len == 0))
        pool_kpe_t = jnp.swapaxes(cache_krope[l:l + 1], 2, 3)
        o = _attn_sample(q_lat, q_pe, lat, kpe, cache_latent[l:l + 1], pool_kpe_t, page_table, pages, streams)
        merged = _merge_sample(o, g_att, yg, w)
        xs = _outmlp(merged, xs, mod_s[2], mod_s[3], mod_s[4], mod_s[5], w, bd, 1024)
        outs[4].append(lat.reshape(bd, 1, KV_LORA))
        outs[5].append(kpe.reshape(bd, 1, QK_ROPE))
        outs[6].append(jnp.stack(conv_rows[1:] + [xr], axis=1))
        outs[7].append(h_new)

    return (xp.reshape(1, s_len, D_MODEL), xs.reshape(bd, 1, D_MODEL)) + tuple(jnp.stack(o) for o in outs)
```

```python
import functools

import jax
import jax.numpy as jnp
from jax import lax
from jax.experimental import pallas as pl
from jax.experimental.pallas import tpu as pltpu

D_MODEL = 1024
D_RNN = 1024
RNN_BLOCKS = 16
RNN_BW = D_RNN // RNN_BLOCKS
CONV_W = 4
LRU_C = 8.0
N_HEADS = 8
QK_NOPE = 128
QK_ROPE = 64
V_DIM = D_MODEL // N_HEADS
Q_LORA = 384
KV_LORA = 256
ROPE_THETA = 10000.0
SM_SCALE = (QK_NOPE + QK_ROPE) ** -0.5
Q_SCALE = SM_SCALE * 1.4426950408889634
D_FF = 4 * D_MODEL
EPS = 1e-6
NEG = -1e30
PAGE_SIZE = 128

LANES = 128
SUBLANES = 8
DECODE_SLOTS = 4
DECODE_AHEAD = DECODE_SLOTS - 1
MXU_DIM = 256
VMEM_LIMIT = 56 * 1024 * 1024

Z_XR = 0
Z_CQ = Z_XR + D_RNN
Z_CKV = Z_CQ + Q_LORA
Z_GR = Z_CKV + KV_LORA
Z_GA = Z_GR + D_MODEL
Z_KPE = Z_GA + D_MODEL
Z_END = Z_KPE + LANES

BF16 = jnp.bfloat16
F32 = jnp.float32


def _rms(x, g):
    return x * lax.rsqrt(jnp.mean(x * x, axis=-1, keepdims=True) + EPS) * g


def _dot(a, b):
    return jnp.dot(a, b, preferred_element_type=F32)


def _dot_nt(a, b):
    return lax.dot_general(a, b, (((1,), (1,)), ((), ())), preferred_element_type=F32)


def _const_spec(shape):
    zeros = (0,) * len(shape)
    return pl.BlockSpec(shape, lambda *_: zeros, pipeline_mode=pl.Buffered(1))


def _params(n_axes=1):
    return pltpu.CompilerParams(dimension_semantics=("arbitrary",) * n_axes, vmem_limit_bytes=VMEM_LIMIT)


def _ada_kernel(c_ref, w_ref, b_ref, o_ref):
    c = c_ref[...]
    o_ref[...] = _dot((c * jax.nn.sigmoid(c)).astype(BF16), w_ref[...]) + b_ref[...]


def _ada(c_all, w_ada, b_ada):
    m = c_all.shape[0]
    tn = D_MODEL
    return pl.pallas_call(
        _ada_kernel,
        grid=(6 * D_MODEL // tn,),
        in_specs=[pl.BlockSpec((m, D_MODEL), lambda j: (0, 0)),
                  pl.BlockSpec((D_MODEL, tn), lambda j: (0, j)),
                  pl.BlockSpec((1, tn), lambda j: (0, j))],
        out_specs=pl.BlockSpec((m, tn), lambda j: (0, j)),
        out_shape=jax.ShapeDtypeStruct((m, 6 * D_MODEL), F32),
        compiler_params=_params(),
        name="ada",
    )(c_all, w_ada, b_ada)


def _swap_halves_64(x):
    n = x.shape[-1]
    lane = lax.broadcasted_iota(jnp.int32, x.shape, x.ndim - 1)
    from_hi = pltpu.roll(x, n - QK_ROPE // 2, x.ndim - 1)
    from_lo = pltpu.roll(x, QK_ROPE // 2, x.ndim - 1)
    return jnp.where(lane % QK_ROPE < QK_ROPE // 2, from_hi, from_lo)


def _inproj_kernel(*refs, scan):
    (x_ref, sh_ref, sc_ref, gpre_ref, win_ref, gq_ref, wuq_ref, wuk_ref, gkv_ref, cos_ref, sin_ref) = refs[:11]
    if scan:
        lru_w = refs[11:18]
        (yg_ref, hlast_ref, xtail_ref, ga_ref, qlat_ref, qpe_ref, lat_ref, latb_ref, kpe_ref, latT_ref, kpeT_ref,
         xp_scr, h_scr) = refs[18:]
    else:
        (xr_ref, gr_ref, ga_ref, qlat_ref, qpe_ref, lat_ref, latb_ref, kpe_ref, latT_ref, kpeT_ref) = refs[11:]
    if scan:
        @pl.when(pl.program_id(0) == 0)
        def _():
            xp_scr[...] = jnp.zeros((SUBLANES, D_RNN), F32)
            h_scr[...] = jnp.zeros((1, D_RNN), F32)
    x = x_ref[...]
    h = _rms(x, gpre_ref[...]) * (1.0 + sc_ref[...]) + sh_ref[...]
    z = _dot(h.astype(BF16), win_ref[...])
    if scan:
        xr = z[:, Z_XR:Z_CQ]
        yg_ref[...], hlast_ref[...] = _rglru_tile(xr, z[:, Z_GR:Z_GA], lru_w, xp_scr, h_scr)
        xtail_ref[...] = xr[xr.shape[0] - SUBLANES:, :]
    else:
        xr_ref[...] = z[:, Z_XR:Z_CQ]
        gr_ref[...] = z[:, Z_GR:Z_GA]
    ga_ref[...] = z[:, Z_GA:Z_KPE]

    q = _dot(_rms(z[:, Z_CQ:Z_CKV], gq_ref[...]).astype(BF16), wuq_ref[...])
    q_rope = q[:, N_HEADS * QK_NOPE:]
    cos, sin = cos_ref[...], sin_ref[...]
    reps = N_HEADS * QK_ROPE // LANES
    q_rope = (q_rope * jnp.concatenate([cos] * reps, axis=1)
              + _swap_halves_64(q_rope) * jnp.concatenate([sin] * reps, axis=1)) * Q_SCALE
    for hd in range(N_HEADS):
        q_nope = q[:, hd * QK_NOPE:(hd + 1) * QK_NOPE].astype(BF16)
        qlat_ref[hd] = (_dot(q_nope, wuk_ref[hd]) * Q_SCALE).astype(BF16)
        qpe_ref[hd] = q_rope[:, hd * QK_ROPE:(hd + 1) * QK_ROPE].astype(BF16)

    lat = _rms(z[:, Z_CKV:Z_GR], gkv_ref[...])
    lat_ref[...] = lat
    latb_ref[...] = lat.astype(BF16)
    latT_ref[0] = lat.T.astype(BF16)
    k_raw = z[:, Z_KPE:Z_END]
    kpe = k_raw * cos + _swap_halves_64(k_raw) * sin
    kpe_ref[...] = kpe[:, :QK_ROPE]
    kpeT_ref[0] = kpe.T[:QK_ROPE, :].astype(BF16)


def _inproj(x, sh, sc, w, rope_tabs, tm, scan):
    m = x.shape[0]
    per_row = sh.shape[0] != 1
    row = lambda i: (i, 0)
    fixed = lambda i: (0, 0)
    mod_spec = pl.BlockSpec((tm, D_MODEL), row) if per_row else _const_spec((1, D_MODEL))
    cos, sin = rope_tabs
    operands = [x, sh, sc, w["g_pre_mix"], w["w_in"], w["g_q"], w["w_uq"], w["w_uk"], w["g_kv"], cos, sin]
    in_specs = [
        pl.BlockSpec((tm, D_MODEL), row), mod_spec, mod_spec,
        _const_spec((1, D_MODEL)), _const_spec((D_MODEL, Z_END)), _const_spec((1, Q_LORA)),
        _const_spec((Q_LORA, N_HEADS * (QK_NOPE + QK_ROPE))), _const_spec((N_HEADS, QK_NOPE, KV_LORA)),
        _const_spec((1, KV_LORA)),
        pl.BlockSpec((tm, LANES), row), pl.BlockSpec((tm, LANES), row),
    ]
    if scan:
        operands += _lru_operands(w)
        in_specs += _lru_specs()
        out_shape = [jax.ShapeDtypeStruct((m, D_MODEL), F32), jax.ShapeDtypeStruct((1, D_RNN), F32),
                     jax.ShapeDtypeStruct((SUBLANES, D_RNN), F32)]
        out_specs = [pl.BlockSpec((tm, D_MODEL), row), pl.BlockSpec((1, D_RNN), fixed),
                     pl.BlockSpec((SUBLANES, D_RNN), fixed)]
        scratch = [pltpu.VMEM((SUBLANES, D_RNN), F32), pltpu.VMEM((1, D_RNN), F32)]
    else:
        out_shape = [jax.ShapeDtypeStruct((m, D_RNN), F32), jax.ShapeDtypeStruct((m, D_MODEL), F32)]
        out_specs = [pl.BlockSpec((tm, D_RNN), row), pl.BlockSpec((tm, D_MODEL), row)]
        scratch = []
    out_shape += [
        jax.ShapeDtypeStruct((m, D_MODEL), F32),
        jax.ShapeDtypeStruct((N_HEADS, m, KV_LORA), BF16), jax.ShapeDtypeStruct((N_HEADS, m, QK_ROPE), BF16),
        jax.ShapeDtypeStruct((m, KV_LORA), F32), jax.ShapeDtypeStruct((m, KV_LORA), BF16),
        jax.ShapeDtypeStruct((m, QK_ROPE), F32),
        jax.ShapeDtypeStruct((m // tm, KV_LORA, tm), BF16), jax.ShapeDtypeStruct((m // tm, QK_ROPE, tm), BF16),
    ]
    out_specs += [
        pl.BlockSpec((tm, D_MODEL), row),
        pl.BlockSpec((N_HEADS, tm, KV_LORA), lambda i: (0, i, 0)),
        pl.BlockSpec((N_HEADS, tm, QK_ROPE), lambda i: (0, i, 0)),
        pl.BlockSpec((tm, KV_LORA), row), pl.BlockSpec((tm, KV_LORA), row),
        pl.BlockSpec((tm, QK_ROPE), row),
        pl.BlockSpec((1, KV_LORA, tm), lambda i: (i, 0, 0)), pl.BlockSpec((1, QK_ROPE, tm), lambda i: (i, 0, 0)),
    ]
    return pl.pallas_call(
        functools.partial(_inproj_kernel, scan=scan), grid=(m // tm,), in_specs=in_specs, out_specs=out_specs,
        out_shape=out_shape, scratch_shapes=scratch, compiler_params=_params(), name="inproj",
    )(*operands)


def _lru_coeffs(xc, wa_ref, ba_ref, wx_ref, bx_ref, lam_ref):
    xb = xc.astype(BF16)
    r_parts, i_parts = [], []
    for j in range(D_RNN // MXU_DIM):
        blk = xb[:, j * MXU_DIM:(j + 1) * MXU_DIM]
        r_parts.append(_dot(blk, wa_ref[j]))
        i_parts.append(_dot(blk, wx_ref[j]))
    r = jax.nn.sigmoid(jnp.concatenate(r_parts, axis=1) + ba_ref[...])
    i = jax.nn.sigmoid(jnp.concatenate(i_parts, axis=1) + bx_ref[...])
    lam = lam_ref[...]
    softplus_neg_lam = jnp.maximum(-lam, 0.0) + jnp.log1p(jnp.exp(-jnp.abs(lam)))
    log_a = -LRU_C * r * softplus_neg_lam
    a = jnp.exp(log_a)
    mult = jnp.sqrt(-jnp.tanh(log_a) * (a * a + 1.0))
    return a, mult, i * xc


def _lru_operands(w):
    return [w["w_conv"], w["b_conv"], w["w_gate_a"], w["b_gate_a"], w["w_gate_x"], w["b_gate_x"], w["lru_lambda"]]


def _lru_specs():
    nblk = D_RNN // MXU_DIM
    return [_const_spec((CONV_W, D_RNN)), _const_spec((1, D_RNN)),
            _const_spec((nblk, MXU_DIM, MXU_DIM)), _const_spec((1, D_RNN)),
            _const_spec((nblk, MXU_DIM, MXU_DIM)), _const_spec((1, D_RNN)), _const_spec((1, D_RNN))]


def _rglru_tile(x, gate, lru_w, xp_scr, h_scr):
    wc_ref, bc_ref, wa_ref, ba_ref, wx_ref, bx_ref, lam_ref = lru_w
    tm = x.shape[0]
    step = pl.program_id(0)

    ng = tm // SUBLANES
    x3 = x.reshape(ng, SUBLANES, D_RNN)
    sub = lax.broadcasted_iota(jnp.int32, (ng, SUBLANES, D_RNN), 1)
    prev_group = xp_scr[...].reshape(1, SUBLANES, D_RNN)
    xp_scr[...] = x3[ng - 1]

    wc = wc_ref[...]
    xc = bc_ref[...] + x3 * wc[CONV_W - 1:CONV_W]
    for k in range(1, CONV_W):
        rot = pltpu.roll(x3, k, 1)
        rot_prev = jnp.concatenate([pltpu.roll(prev_group, k, 1), rot[:ng - 1]], axis=0)
        xc = xc + jnp.where(sub >= k, rot, rot_prev) * wc[CONV_W - 1 - k:CONV_W - k]
    xc = xc.reshape(tm, D_RNN)

    a, mult, gated = _lru_coeffs(xc, wa_ref, ba_ref, wx_ref, bx_ref, lam_ref)
    rows = lax.broadcasted_iota(jnp.int32, (tm, D_RNN), 0)
    mult = jnp.where(rows + step * tm == 0, 1.0, mult)
    a = a.reshape(ng, SUBLANES, D_RNN)
    b = (mult * gated).reshape(ng, SUBLANES, D_RNN)

    d = 1
    while d < SUBLANES:
        keep = sub >= d
        b = jnp.where(keep, a * pltpu.roll(b, d, 1) + b, b)
        a = jnp.where(keep, a * pltpu.roll(a, d, 1), a)
        d *= 2
    h_prev = h_scr[...]
    hs = []
    for g in range(ng):
        hg = a[g] * h_prev + b[g]
        hs.append(hg)
        h_prev = hg[SUBLANES - 1:SUBLANES, :]
    h_scr[...] = h_prev
    return jax.nn.sigmoid(gate) * jnp.concatenate(hs, axis=0), h_prev


def _rglru_sample_kernel(xr_ref, g_ref, s0_ref, s1_ref, s2_ref, h0_ref, wc_ref, bc_ref, wa_ref, ba_ref,
                         wx_ref, bx_ref, lam_ref, yg_ref, h_ref, *, at_start):
    wc = wc_ref[...]
    xc = (bc_ref[...] + xr_ref[...] * wc[3:4] + s2_ref[...] * wc[2:3] + s1_ref[...] * wc[1:2]
          + s0_ref[...] * wc[0:1])
    a, mult, gated = _lru_coeffs(xc, wa_ref, ba_ref, wx_ref, bx_ref, lam_ref)
    if at_start:
        mult = jnp.ones_like(mult)
    h = a * h0_ref[...] + mult * gated
    h_ref[...] = h
    yg_ref[...] = jax.nn.sigmoid(g_ref[...]) * h


def _rglru_sample(xr, g_rnn, conv_rows, h0, w, at_start):
    m = xr.shape[0]
    full = _const_spec((m, D_RNN))
    return pl.pallas_call(
        functools.partial(_rglru_sample_kernel, at_start=at_start), grid=(1,), in_specs=[full] * 6 + _lru_specs(),
        out_specs=[full, full],
        out_shape=[jax.ShapeDtypeStruct((m, D_MODEL), F32), jax.ShapeDtypeStruct((m, D_RNN), F32)],
        compiler_params=_params(), name="rglru_sample",
    )(xr, g_rnn, *conv_rows, h0, *_lru_operands(w))


def _value_up_and_merge(o_of_head, ga_ref, yg_ref, wuv_ref, out_ref):
    for hd in range(N_HEADS):
        cols = slice(hd * V_DIM, (hd + 1) * V_DIM)
        y = _dot(o_of_head(hd).astype(BF16), wuv_ref[hd])
        out_ref[:, cols] = (jax.nn.sigmoid(ga_ref[:, cols]) * y + yg_ref[:, cols]).astype(out_ref.dtype)


def _attn_prompt_kernel(qlat_ref, qpe_ref, latT_ref, kpeT_ref, lat_ref, ga_ref, yg_ref, wuv_ref, out_ref,
                        s0_scr, s1_scr, m_scr, l_scr, acc_scr, *, tq, tk, tc, hg):
    qi = pl.program_id(0)
    rows = N_HEADS * tq
    grp = hg * tq
    n_sub = tk // tc
    m_scr[...] = jnp.full((rows, LANES), NEG, F32)
    l_scr[...] = jnp.zeros((rows, LANES), F32)
    acc_scr[...] = jnp.zeros((rows, KV_LORA), F32)

    groups = range(N_HEADS // hg)

    def scores_into(s_scr, j, g):
        q_lat = qlat_ref[g * hg:(g + 1) * hg].reshape(grp, KV_LORA)
        q_pe = qpe_ref[g * hg:(g + 1) * hg].reshape(grp, QK_ROPE)
        for c in range(n_sub):
            s_scr[g * grp:(g + 1) * grp, c * tc:(c + 1) * tc] = (
                _dot(q_lat, latT_ref[j * n_sub + c]) + _dot(q_pe, kpeT_ref[j * n_sub + c]))

    def process(s_scr, j, g, masked):
        start = pl.multiple_of(j * tk, tk)
        r = slice(g * grp, (g + 1) * grp)
        s = s_scr[r, :]
        if masked:
            qpos = qi * tq + lax.broadcasted_iota(jnp.int32, (grp, tk), 0) % tq
            kpos = start + lax.broadcasted_iota(jnp.int32, (grp, tk), 1)
            s = jnp.where(kpos <= qpos, s, NEG)
        m_prev = m_scr[r, :]
        m_new = jnp.maximum(m_prev, jnp.max(s, axis=1, keepdims=True))
        alpha = jnp.exp2(m_prev - m_new)
        p = jnp.exp2(s - jnp.concatenate([m_new] * (tk // LANES), axis=1))
        lane_sums = p[:, :LANES]
        for c in range(1, tk // LANES):
            lane_sums = lane_sums + p[:, c * LANES:(c + 1) * LANES]
        l_scr[r, :] = alpha * l_scr[r, :] + lane_sums
        acc_scr[r, :] = (jnp.concatenate([alpha] * (KV_LORA // LANES), axis=1) * acc_scr[r, :]
                         + _dot(p.astype(BF16), lat_ref[pl.ds(start, tk), :]))
        m_scr[r, :] = m_new

    def step(cur_scr, j, nxt_scr=None, masked=False):
        for g in groups:
            if nxt_scr is not None:
                scores_into(nxt_scr, j + 1, g)
            process(cur_scr, j, g, masked)

    n_full = (qi * tq) // tk
    for g in groups:
        scores_into(s0_scr, 0, g)

    def body(t, carry):
        step(s0_scr, 2 * t, s1_scr)
        step(s1_scr, 2 * t + 1, s0_scr)
        return carry

    lax.fori_loop(0, n_full // 2, body, 0)

    @pl.when(n_full % 2 == 0)
    def _():
        step(s0_scr, n_full, masked=True)

    @pl.when(n_full % 2 == 1)
    def _():
        step(s0_scr, n_full - 1, s1_scr)
        step(s1_scr, n_full, masked=True)

    acc_scr[...] = acc_scr[...] / jnp.sum(l_scr[...], axis=1, keepdims=True)
    _value_up_and_merge(lambda hd: acc_scr[hd * tq:(hd + 1) * tq, :], ga_ref, yg_ref, wuv_ref, out_ref)


def _attn_prompt(q_lat, q_pe, lat_t, kpe_t, lat_b, g_att, yg, w, tq, tk, hg):
    s_len = lat_b.shape[0]
    n_col, _, tc = lat_t.shape
    row = lambda i: (i, 0)
    in_specs = [pl.BlockSpec((N_HEADS, tq, KV_LORA), lambda i: (0, i, 0)),
                pl.BlockSpec((N_HEADS, tq, QK_ROPE), lambda i: (0, i, 0)),
                _const_spec((n_col, KV_LORA, tc)), _const_spec((n_col, QK_ROPE, tc)),
                _const_spec((s_len, KV_LORA)),
                pl.BlockSpec((tq, D_MODEL), row), pl.BlockSpec((tq, D_MODEL), row),
                _const_spec((N_HEADS, KV_LORA, V_DIM))]
    rows = N_HEADS * tq
    return pl.pallas_call(
        functools.partial(_attn_prompt_kernel, tq=tq, tk=tk, tc=tc, hg=hg), grid=(s_len // tq,),
        in_specs=in_specs, out_specs=pl.BlockSpec((tq, D_MODEL), row),
        out_shape=jax.ShapeDtypeStruct((s_len, D_MODEL), BF16),
        scratch_shapes=[pltpu.VMEM((rows, tk), F32), pltpu.VMEM((rows, tk), F32),
                        pltpu.VMEM((rows, LANES), F32), pltpu.VMEM((rows, LANES), F32),
                        pltpu.VMEM((rows, KV_LORA), F32)],
        compiler_params=_params(), name="attn_prompt",
    )(q_lat, q_pe, lat_t, kpe_t, lat_b, g_att, yg, w["w_uv"])


def _attn_sample_kernel(pt_ref, qlat_ref, qpe_ref, latn_ref, kpen_ref, lat_hbm, kpt_hbm, o_ref,
                        lat_buf, kpt_buf, lat_sem, kpt_sem, m_scr, l_scr, acc_scr, *, pages, streams):
    b, c = pl.program_id(0), pl.program_id(1)
    n_c = pl.num_programs(1)
    last_step = pl.num_programs(0) * n_c - 1
    step = b * n_c + c
    q_lat = qlat_ref[:, 0, :]
    q_pe = qpe_ref[:, 0, :]
    per = pages // streams

    def page_copies(row, chunk, slot, i):
        page = pt_ref[row, chunk * pages + i]
        return (pltpu.make_async_copy(lat_hbm.at[0, page], lat_buf.at[slot, i], lat_sem.at[slot, i]),
                pltpu.make_async_copy(kpt_hbm.at[0, page], kpt_buf.at[slot, i], kpt_sem.at[slot, i]))

    def start_chunk(row, chunk, slot, i):
        for cp in page_copies(row, chunk, slot, i):
            cp.start()

    def wait_chunk(row, chunk, slot):
        for i in range(pages):
            for cp in page_copies(row, chunk, slot, i):
                cp.wait()

    def consume(slot, prefetch):
        kls, s_parts = [], []
        for i in range(pages):
            prefetch(i)
            kl = lat_buf[slot, i].astype(BF16)
            kls.append(kl)
            s_parts.append(_dot_nt(q_lat, kl) + _dot(q_pe, kpt_buf[slot, i].astype(BF16)))
        probs, alphas = [], []
        for st in range(streams):
            s = jnp.concatenate(s_parts[st * per:(st + 1) * per], axis=1)
            m_prev = m_scr[st]
            m_new = jnp.maximum(m_prev, jnp.max(s, axis=1, keepdims=True))
            alpha = jnp.exp2(m_prev - m_new)
            p = jnp.exp2(s - m_new)
            l_scr[st] = alpha * l_scr[st] + jnp.sum(p, axis=1, keepdims=True)
            m_scr[st] = m_new
            probs.append(p.astype(BF16))
            alphas.append(alpha)
        for st in range(streams):
            pv = _dot(probs[st][:, :PAGE_SIZE], kls[st * per])
            for i in range(1, per):
                pv += _dot(probs[st][:, i * PAGE_SIZE:(i + 1) * PAGE_SIZE], kls[st * per + i])
            acc_scr[st] = alphas[st] * acc_scr[st] + pv

    @pl.when(c == 0)
    def _():
        m_scr[...] = jnp.full((streams, N_HEADS, 1), NEG, F32)
        l_scr[...] = jnp.zeros((streams, N_HEADS, 1), F32)
        acc_scr[...] = jnp.zeros((streams, N_HEADS, KV_LORA), F32)

    cpr = n_c * DECODE_SLOTS
    last_chunk = pl.num_programs(0) * cpr - 1
    first = step * DECODE_SLOTS

    def locate(k):
        return k // cpr, k % cpr

    @pl.when(step == 0)
    def _():
        for k in range(DECODE_AHEAD):
            for i in range(pages):
                start_chunk(*locate(k), k, i)

    for j in range(DECODE_SLOTS):
        ahead = locate(jnp.minimum(first + j + DECODE_AHEAD, last_chunk))
        ahead_slot = (j + DECODE_AHEAD) % DECODE_SLOTS
        wait_chunk(*locate(first + j), j)
        consume(j, lambda i, ahead=ahead, ahead_slot=ahead_slot: start_chunk(*ahead, ahead_slot, i))

    @pl.when(step == last_step)
    def _():
        for j in range(DECODE_SLOTS - DECODE_AHEAD, DECODE_SLOTS):
            wait_chunk(*locate(last_chunk), (j + DECODE_AHEAD) % DECODE_SLOTS)

    @pl.when(c == n_c - 1)
    def _():
        lat_n = latn_ref[...]
        s_n = (jnp.sum(q_lat.astype(F32) * lat_n, axis=1, keepdims=True)
               + jnp.sum(q_pe.astype(F32) * kpen_ref[...], axis=1, keepdims=True))
        m_fin = s_n
        for st in range(streams):
            m_fin = jnp.maximum(m_fin, m_scr[st])
        p_n = jnp.exp2(s_n - m_fin)
        num, den = p_n * lat_n, p_n
        for st in range(streams):
            corr = jnp.exp2(m_scr[st] - m_fin)
            num = num + acc_scr[st] * corr
            den = den + l_scr[st] * corr
        o_ref[:, 0, :] = num / den


def _attn_sample(q_lat, q_pe, lat_new, kpe_new, pool_lat, pool_kpe_t, page_table, pages, streams):
    bd, n_pages = page_table.shape
    assert n_pages % (DECODE_SLOTS * pages) == 0, "each grid step consumes DECODE_SLOTS chunks of pages"
    per_b = lambda b, c, pt: (0, b, 0, 0)
    in_specs = [pl.BlockSpec((N_HEADS, None, 1, KV_LORA), per_b), pl.BlockSpec((N_HEADS, None, 1, QK_ROPE), per_b),
                pl.BlockSpec((None, 1, KV_LORA), lambda b, c, pt: (b, 0, 0)),
                pl.BlockSpec((None, 1, QK_ROPE), lambda b, c, pt: (b, 0, 0)),
                pl.BlockSpec(memory_space=pl.ANY), pl.BlockSpec(memory_space=pl.ANY)]
    grid_spec = pltpu.PrefetchScalarGridSpec(
        num_scalar_prefetch=1, grid=(bd, n_pages // (DECODE_SLOTS * pages)), in_specs=in_specs,
        out_specs=pl.BlockSpec((N_HEADS, None, 1, KV_LORA), per_b),
        scratch_shapes=[pltpu.VMEM((DECODE_SLOTS, pages, PAGE_SIZE, KV_LORA), F32),
                        pltpu.VMEM((DECODE_SLOTS, pages, QK_ROPE, PAGE_SIZE), F32),
                        pltpu.SemaphoreType.DMA((DECODE_SLOTS, pages)),
                        pltpu.SemaphoreType.DMA((DECODE_SLOTS, pages)),
                        pltpu.VMEM((streams, N_HEADS, 1), F32), pltpu.VMEM((streams, N_HEADS, 1), F32),
                        pltpu.VMEM((streams, N_HEADS, KV_LORA), F32)])
    o = pl.pallas_call(
        functools.partial(_attn_sample_kernel, pages=pages, streams=streams), grid_spec=grid_spec,
        out_shape=jax.ShapeDtypeStruct((N_HEADS, bd, 1, KV_LORA), F32),
        compiler_params=_params(2), name="attn_sample",
    )(page_table, q_lat.reshape(N_HEADS, bd, 1, KV_LORA), q_pe.reshape(N_HEADS, bd, 1, QK_ROPE),
      lat_new.reshape(bd, 1, KV_LORA), kpe_new.reshape(bd, 1, QK_ROPE), pool_lat, pool_kpe_t)
    return o.reshape(N_HEADS, bd, KV_LORA)


def _merge_sample_kernel(o_ref, ga_ref, yg_ref, wuv_ref, out_ref):
    _value_up_and_merge(lambda hd: o_ref[hd], ga_ref, yg_ref, wuv_ref, out_ref)


def _merge_sample(o, g_att, yg, w):
    m = g_att.shape[0]
    return pl.pallas_call(
        _merge_sample_kernel, grid=(1,),
        in_specs=[_const_spec((N_HEADS, m, KV_LORA)), _const_spec((m, D_MODEL)), _const_spec((m, D_MODEL)),
                  _const_spec((N_HEADS, KV_LORA, V_DIM))],
        out_specs=_const_spec((m, D_MODEL)),
        out_shape=jax.ShapeDtypeStruct((m, D_MODEL), BF16),
        compiler_params=_params(), name="merge_sample",
    )(o, g_att, yg, w["w_uv"])


def _outmlp_kernel(mg_ref, x_ref, gt1_ref, sh2_ref, sc2_ref, gt2_ref, gpm_ref, gpre_ref, gpost_ref,
                   wo_ref, wup_ref, wdn_ref, y_ref, *, ff_chunk):
    x1 = x_ref[...] + gt1_ref[...] * _rms(_dot(mg_ref[...], wo_ref[...]), gpm_ref[...])
    h2 = (_rms(x1, gpre_ref[...]) * (1.0 + sc2_ref[...]) + sh2_ref[...]).astype(BF16)
    f = None
    for j in range(D_FF // ff_chunk):
        cols = slice(j * ff_chunk, (j + 1) * ff_chunk)
        u = jnp.maximum(_dot(h2, wup_ref[:, cols]), 0.0)
        part = _dot((u * u).astype(BF16), wdn_ref[cols, :])
        f = part if f is None else f + part
    y_ref[...] = x1 + gt2_ref[...] * _rms(f, gpost_ref[...])


def _outmlp(merged, x, gt1, sh2, sc2, gt2, w, tm, ff_chunk):
    m = x.shape[0]
    per_row = gt1.shape[0] != 1
    row = lambda i: (i, 0)
    mod_spec = pl.BlockSpec((tm, D_MODEL), row) if per_row else _const_spec((1, D_MODEL))
    vec = _const_spec((1, D_MODEL))
    in_specs = [pl.BlockSpec((tm, D_MODEL), row), pl.BlockSpec((tm, D_MODEL), row),
                mod_spec, mod_spec, mod_spec, mod_spec, vec, vec, vec,
                _const_spec((D_MODEL, D_MODEL)), _const_spec((D_MODEL, D_FF)), _const_spec((D_FF, D_MODEL))]
    return pl.pallas_call(
        functools.partial(_outmlp_kernel, ff_chunk=ff_chunk), grid=(m // tm,), in_specs=in_specs,
        out_specs=pl.BlockSpec((tm, D_MODEL), row),
        out_shape=jax.ShapeDtypeStruct((m, D_MODEL), F32),
        compiler_params=_params(), name="outmlp",
    )(merged, x, gt1, sh2, sc2, gt2, w["g_post_mix"], w["g_pre_mlp"], w["g_post_mlp"],
      w["w_o"], w["w_up"], w["w_down"])


def _rope_tables(pos):
    half = QK_ROPE // 2
    inv = 1.0 / (ROPE_THETA ** (jnp.arange(half, dtype=F32) / half))
    ang = pos.astype(F32)[:, None] * jnp.tile(inv, 2 * LANES // QK_ROPE)[None, :]
    sign = jnp.tile(jnp.concatenate([-jnp.ones((half,), F32), jnp.ones((half,), F32)]), LANES // QK_ROPE)
    return jnp.cos(ang), jnp.sin(ang) * sign[None, :]


def _block_diag_tiles(wg):
    per = MXU_DIM // RNN_BW
    wg = wg.reshape(D_RNN // MXU_DIM, per, RNN_BW, RNN_BW)
    eye = jnp.eye(per, dtype=wg.dtype)
    return jnp.einsum("tpij,pq->tpiqj", wg, eye).reshape(D_RNN // MXU_DIM, MXU_DIM, MXU_DIM).astype(BF16)


def _layer_weights(l, w_in, w_conv, b_conv, w_gate_a, b_gate_a, w_gate_x, b_gate_x, lru_lambda, g_q, w_uq,
                   g_kv, w_uk, w_uv, w_o, w_up, w_down, g_pre_mix, g_post_mix, g_pre_mlp, g_post_mlp):
    wi = w_in[l]
    s_ckv = D_RNN + Q_LORA + KV_LORA
    s_kpe = s_ckv + QK_ROPE
    w_in_r = jnp.concatenate([wi[:, :s_ckv], wi[:, s_kpe:], wi[:, s_ckv:s_kpe],
                              jnp.zeros((D_MODEL, LANES - QK_ROPE), wi.dtype)], axis=1).astype(BF16)
    uq = w_uq[l]
    w_uq_r = jnp.concatenate([uq[:, :, :QK_NOPE].reshape(Q_LORA, N_HEADS * QK_NOPE),
                              uq[:, :, QK_NOPE:].reshape(Q_LORA, N_HEADS * QK_ROPE)], axis=1).astype(BF16)
    vec = lambda v: v[l].reshape(1, -1)
    return {
        "w_in": w_in_r, "w_uq": w_uq_r,
        "w_uk": jnp.transpose(w_uk[l], (1, 2, 0)).astype(BF16),
        "w_uv": jnp.transpose(w_uv[l], (1, 0, 2)).astype(BF16),
        "w_o": w_o[l].astype(BF16), "w_up": w_up[l].astype(BF16), "w_down": w_down[l].astype(BF16),
        "w_gate_a": _block_diag_tiles(w_gate_a[l]), "w_gate_x": _block_diag_tiles(w_gate_x[l]),
        "w_conv": w_conv[l], "b_conv": vec(b_conv), "b_gate_a": vec(b_gate_a), "b_gate_x": vec(b_gate_x),
        "lru_lambda": vec(lru_lambda), "g_q": vec(g_q), "g_kv": vec(g_kv), "g_pre_mix": vec(g_pre_mix),
        "g_post_mix": vec(g_post_mix), "g_pre_mlp": vec(g_pre_mlp), "g_post_mlp": vec(g_post_mlp),
    }


def _tile(n, pref):
    return pref if n % pref == 0 else n


def kernel(x_prompt, x_sample, c_prompt, c_sample, cache_latent, cache_krope, state_conv, state_rnn, page_table,
           w_ada, b_ada, g_pre_mix, g_post_mix, g_pre_mlp, g_post_mlp, w_in, w_conv, b_conv, w_gate_a, b_gate_a,
           w_gate_x, b_gate_x, lru_lambda, g_q, w_uq, g_kv, w_uk, w_uv, w_o, w_up, w_down):
    bp, s_len, _ = x_prompt.shape
    bd, t_dec, _ = x_sample.shape
    depth = w_in.shape[0]
    assert bp == 1 and t_dec == 1, "one prompt sequence and one new token per decode row"
    n_pages = page_table.shape[1]
    past_len = n_pages * PAGE_SIZE
    pages = next(p for p in (16, 8, 4, 2, 1) if n_pages % (DECODE_SLOTS * p) == 0)
    streams = min(pages, 4)

    xp = x_prompt.reshape(s_len, D_MODEL)
    xs = x_sample.reshape(bd, D_MODEL)
    c_all = jnp.concatenate([c_prompt, jnp.zeros((7, D_MODEL), F32), c_sample], axis=0)
    tabs_p = _rope_tables(jnp.arange(s_len))
    tabs_s = _rope_tables(jnp.full((bd,), past_len))
    tm_in, tm_out = _tile(s_len, 256), _tile(s_len, 512)
    tq, tk, hg = _tile(s_len, 128), _tile(s_len, 512), 2

    outs = [[] for _ in range(8)]
    for l in range(depth):
        w = _layer_weights(l, w_in, w_conv, b_conv, w_gate_a, b_gate_a, w_gate_x, b_gate_x, lru_lambda, g_q,
                           w_uq, g_kv, w_uk, w_uv, w_o, w_up, w_down, g_pre_mix, g_post_mix, g_pre_mlp,
                           g_post_mlp)
        mod = _ada(c_all, w_ada[l].astype(BF16), b_ada[l].reshape(1, -1))
        mod_p = [mod[0:1, k * D_MODEL:(k + 1) * D_MODEL] for k in range(6)]
        mod_s = [mod[8:, k * D_MODEL:(k + 1) * D_MODEL] for k in range(6)]

        yg, h_last, x_tail, g_att, q_lat, q_pe, lat, lat_b, kpe, lat_t, kpe_t = _inproj(
            xp, mod_p[0], mod_p[1], w, tabs_p, tm_in, scan=True)
        merged = _attn_prompt(q_lat, q_pe, lat_t, kpe_t, lat_b, g_att, yg, w, tq, tk, hg)
        xp = _outmlp(merged, xp, mod_p[2], mod_p[3], mod_p[4], mod_p[5], w, tm_out, 1024)
        outs[0].append(lat.reshape(1, s_len, KV_LORA))
        outs[1].append(kpe.reshape(1, s_len, QK_ROPE))
        outs[2].append(x_tail[SUBLANES - (CONV_W - 1):].reshape(1, CONV_W - 1, D_RNN))
        outs[3].append(h_last)

        xr, g_rnn, g_att, q_lat, q_pe, lat, _, kpe, _, _ = _inproj(
            xs, mod_s[0], mod_s[1], w, tabs_s, bd, scan=False)
        conv_rows = [state_conv[l][:, k, :] for k in range(CONV_W - 1)]
        yg, h_new = _rglru_sample(xr, g_rnn, conv_rows, state_rnn[l], w, at_start=(past_len == 0))
        pool_kpe_t = jnp.swapaxes(cache_krope[l:l + 1], 2, 3)
        o = _attn_sample(q_lat, q_pe, lat, kpe, cache_latent[l:l + 1], pool_kpe_t, page_table, pages, streams)
        merged = _merge_sample(o, g_att, yg, w)
        xs = _outmlp(merged, xs, mod_s[2], mod_s[3], mod_s[4], mod_s[5], w, bd, 1024)
        outs[4].append(lat.reshape(bd, 1, KV_LORA))
        outs[5].append(kpe.reshape(bd, 1, QK_ROPE))
        outs[6].append(jnp.stack(conv_rows[1:] + [xr], axis=1))
        outs[7].append(h_new)

    return (xp.reshape(1, s_len, D_MODEL), xs.reshape(bd, 1, D_MODEL)) + tuple(jnp.stack(o) for o in outs)
```

```python
import functools

import jax
import jax.numpy as jnp
from jax import lax
from jax.experimental import pallas as pl
from jax.experimental.pallas import tpu as pltpu

D_MODEL = 1024
D_RNN = 1024
RNN_BLOCKS = 16
RNN_BW = D_RNN // RNN_BLOCKS
CONV_W = 4
LRU_C = 8.0
N_HEADS = 8
QK_NOPE = 128
QK_ROPE = 64
V_DIM = D_MODEL // N_HEADS
Q_LORA = 384
KV_LORA = 256
ROPE_THETA = 10000.0
SM_SCALE = (QK_NOPE + QK_ROPE) ** -0.5
Q_SCALE = SM_SCALE * 1.4426950408889634
D_FF = 4 * D_MODEL
EPS = 1e-6
NEG = -1e30
PAGE_SIZE = 128

LANES = 128
SUBLANES = 8
DECODE_SLOTS = 4
DECODE_AHEAD = DECODE_SLOTS - 1
MXU_DIM = 256
VMEM_LIMIT = 56 * 1024 * 1024

Z_XR = 0
Z_CQ = Z_XR + D_RNN
Z_CKV = Z_CQ + Q_LORA
Z_GR = Z_CKV + KV_LORA
Z_GA = Z_GR + D_MODEL
Z_KPE = Z_GA + D_MODEL
Z_END = Z_KPE + LANES

BF16 = jnp.bfloat16
F32 = jnp.float32


def _rms(x, g):
    return x * lax.rsqrt(jnp.mean(x * x, axis=-1, keepdims=True) + EPS) * g


def _dot(a, b):
    return jnp.dot(a, b, preferred_element_type=F32)


def _dot_nt(a, b):
    return lax.dot_general(a, b, (((1,), (1,)), ((), ())), preferred_element_type=F32)


def _const_spec(shape):
    zeros = (0,) * len(shape)
    return pl.BlockSpec(shape, lambda *_: zeros, pipeline_mode=pl.Buffered(1))


def _params(n_axes=1):
    return pltpu.CompilerParams(dimension_semantics=("arbitrary",) * n_axes, vmem_limit_bytes=VMEM_LIMIT)


def _ada_kernel(c_ref, w_ref, b_ref, o_ref):
    c = c_ref[...]
    o_ref[...] = _dot((c * jax.nn.sigmoid(c)).astype(BF16), w_ref[...]) + b_ref[...]


def _ada(c_all, w_ada, b_ada):
    m = c_all.shape[0]
    tn = D_MODEL
    return pl.pallas_call(
        _ada_kernel,
        grid=(6 * D_MODEL // tn,),
        in_specs=[pl.BlockSpec((m, D_MODEL), lambda j: (0, 0)),
                  pl.BlockSpec((D_MODEL, tn), lambda j: (0, j)),
                  pl.BlockSpec((1, tn), lambda j: (0, j))],
        out_specs=pl.BlockSpec((m, tn), lambda j: (0, j)),
        out_shape=jax.ShapeDtypeStruct((m, 6 * D_MODEL), F32),
        compiler_params=_params(),
        name="ada",
    )(c_all, w_ada, b_ada)


def _swap_halves_64(x):
    n = x.shape[-1]
    lane = lax.broadcasted_iota(jnp.int32, x.shape, x.ndim - 1)
    from_hi = pltpu.roll(x, n - QK_ROPE // 2, x.ndim - 1)
    from_lo = pltpu.roll(x, QK_ROPE // 2, x.ndim - 1)
    return jnp.where(lane % QK_ROPE < QK_ROPE // 2, from_hi, from_lo)


def _inproj_kernel(*refs, scan):
    (x_ref, sh_ref, sc_ref, gpre_ref, win_ref, gq_ref, wuq_ref, wuk_ref, gkv_ref, cos_ref, sin_ref) = refs[:11]
    if scan:
        lru_w = refs[11:18]
        (yg_ref, hlast_ref, xtail_ref, ga_ref, qlat_ref, qpe_ref, lat_ref, latb_ref, kpe_ref, latT_ref, kpeT_ref,
         xp_scr, h_scr) = refs[18:]
    else:
        (xr_ref, gr_ref, ga_ref, qlat_ref, qpe_ref, lat_ref, latb_ref, kpe_ref, latT_ref, kpeT_ref) = refs[11:]
    if scan:
        @pl.when(pl.program_id(0) == 0)
        def _():
            xp_scr[...] = jnp.zeros((SUBLANES, D_RNN), F32)
            h_scr[...] = jnp.zeros((1, D_RNN), F32)
    x = x_ref[...]
    h = _rms(x, gpre_ref[...]) * (1.0 + sc_ref[...]) + sh_ref[...]
    z = _dot(h.astype(BF16), win_ref[...])
    if scan:
        xr = z[:, Z_XR:Z_CQ]
        yg_ref[...], hlast_ref[...] = _rglru_tile(xr, z[:, Z_GR:Z_GA], lru_w, xp_scr, h_scr)
        xtail_ref[...] = xr[xr.shape[0] - SUBLANES:, :]
    else:
        xr_ref[...] = z[:, Z_XR:Z_CQ]
        gr_ref[...] = z[:, Z_GR:Z_GA]
    ga_ref[...] = z[:, Z_GA:Z_KPE]

    q = _dot(_rms(z[:, Z_CQ:Z_CKV], gq_ref[...]).astype(BF16), wuq_ref[...])
    q_rope = q[:, N_HEADS * QK_NOPE:]
    cos, sin = cos_ref[...], sin_ref[...]
    reps = N_HEADS * QK_ROPE // LANES
    q_rope = (q_rope * jnp.concatenate([cos] * reps, axis=1)
              + _swap_halves_64(q_rope) * jnp.concatenate([sin] * reps, axis=1)) * Q_SCALE
    for hd in range(N_HEADS):
        q_nope = q[:, hd * QK_NOPE:(hd + 1) * QK_NOPE].astype(BF16)
        qlat_ref[hd] = (_dot(q_nope, wuk_ref[hd]) * Q_SCALE).astype(BF16)
        qpe_ref[hd] = q_rope[:, hd * QK_ROPE:(hd + 1) * QK_ROPE].astype(BF16)

    lat = _rms(z[:, Z_CKV:Z_GR], gkv_ref[...])
    lat_ref[...] = lat
    latb_ref[...] = lat.astype(BF16)
    latT_ref[0] = lat.T.astype(BF16)
    k_raw = z[:, Z_KPE:Z_END]
    kpe = k_raw * cos + _swap_halves_64(k_raw) * sin
    kpe_ref[...] = kpe[:, :QK_ROPE]
    kpeT_ref[0] = kpe.T[:QK_ROPE, :].astype(BF16)


def _inproj(x, sh, sc, w, rope_tabs, tm, scan):
    m = x.shape[0]
    per_row = sh.shape[0] != 1
    row = lambda i: (i, 0)
    fixed = lambda i: (0, 0)
    mod_spec = pl.BlockSpec((tm, D_MODEL), row) if per_row else _const_spec((1, D_MODEL))
    cos, sin = rope_tabs
    operands = [x, sh, sc, w["g_pre_mix"], w["w_in"], w["g_q"], w["w_uq"], w["w_uk"], w["g_kv"], cos, sin]
    in_specs = [
        pl.BlockSpec((tm, D_MODEL), row), mod_spec, mod_spec,
        _const_spec((1, D_MODEL)), _const_spec((D_MODEL, Z_END)), _const_spec((1, Q_LORA)),
        _const_spec((Q_LORA, N_HEADS * (QK_NOPE + QK_ROPE))), _const_spec((N_HEADS, QK_NOPE, KV_LORA)),
        _const_spec((1, KV_LORA)),
        pl.BlockSpec((tm, LANES), row), pl.BlockSpec((tm, LANES), row),
    ]
    if scan:
        operands += _lru_operands(w)
        in_specs += _lru_specs()
        out_shape = [jax.ShapeDtypeStruct((m, D_MODEL), F32), jax.ShapeDtypeStruct((1, D_RNN), F32),
                     jax.ShapeDtypeStruct((SUBLANES, D_RNN), F32)]
        out_specs = [pl.BlockSpec((tm, D_MODEL), row), pl.BlockSpec((1, D_RNN), fixed),
                     pl.BlockSpec((SUBLANES, D_RNN), fixed)]
        scratch = [pltpu.VMEM((SUBLANES, D_RNN), F32), pltpu.VMEM((1, D_RNN), F32)]
    else:
        out_shape = [jax.ShapeDtypeStruct((m, D_RNN), F32), jax.ShapeDtypeStruct((m, D_MODEL), F32)]
        out_specs = [pl.BlockSpec((tm, D_RNN), row), pl.BlockSpec((tm, D_MODEL), row)]
        scratch = []
    out_shape += [
        jax.ShapeDtypeStruct((m, D_MODEL), F32),
        jax.ShapeDtypeStruct((N_HEADS, m, KV_LORA), BF16), jax.ShapeDtypeStruct((N_HEADS, m, QK_ROPE), BF16),
        jax.ShapeDtypeStruct((m, KV_LORA), F32), jax.ShapeDtypeStruct((m, KV_LORA), BF16),
        jax.ShapeDtypeStruct((m, QK_ROPE), F32),
        jax.ShapeDtypeStruct((m // tm, KV_LORA, tm), BF16), jax.ShapeDtypeStruct((m // tm, QK_ROPE, tm), BF16),
    ]
    out_specs += [
        pl.BlockSpec((tm, D_MODEL), row),
        pl.BlockSpec((N_HEADS, tm, KV_LORA), lambda i: (0, i, 0)),
        pl.BlockSpec((N_HEADS, tm, QK_ROPE), lambda i: (0, i, 0)),
        pl.BlockSpec((tm, KV_LORA), row), pl.BlockSpec((tm, KV_LORA), row),
        pl.BlockSpec((tm, QK_ROPE), row),
        pl.BlockSpec((1, KV_LORA, tm), lambda i: (i, 0, 0)), pl.BlockSpec((1, QK_ROPE, tm), lambda i: (i, 0, 0)),
    ]
    return pl.pallas_call(
        functools.partial(_inproj_kernel, scan=scan), grid=(m // tm,), in_specs=in_specs, out_specs=out_specs,
        out_shape=out_shape, scratch_shapes=scratch, compiler_params=_params(), name="inproj",
    )(*operands)


def _lru_coeffs(xc, wa_ref, ba_ref, wx_ref, bx_ref, lam_ref):
    xb = xc.astype(BF16)
    r_parts, i_parts = [], []
    for j in range(D_RNN // MXU_DIM):
        blk = xb[:, j * MXU_DIM:(j + 1) * MXU_DIM]
        r_parts.append(_dot(blk, wa_ref[j]))
        i_parts.append(_dot(blk, wx_ref[j]))
    r = jax.nn.sigmoid(jnp.concatenate(r_parts, axis=1) + ba_ref[...])
    i = jax.nn.sigmoid(jnp.concatenate(i_parts, axis=1) + bx_ref[...])
    lam = lam_ref[...]
    softplus_neg_lam = jnp.maximum(-lam, 0.0) + jnp.log1p(jnp.exp(-jnp.abs(lam)))
    log_a = -LRU_C * r * softplus_neg_lam
    a = jnp.exp(log_a)
    mult = jnp.sqrt(-jnp.tanh(log_a) * (a * a + 1.0))
    return a, mult, i * xc


def _lru_operands(w):
    return [w["w_conv"], w["b_conv"], w["w_gate_a"], w["b_gate_a"], w["w_gate_x"], w["b_gate_x"], w["lru_lambda"]]


def _lru_specs():
    nblk = D_RNN // MXU_DIM
    return [_const_spec((CONV_W, D_RNN)), _const_spec((1, D_RNN)),
            _const_spec((nblk, MXU_DIM, MXU_DIM)), _const_spec((1, D_RNN)),
            _const_spec((nblk, MXU_DIM, MXU_DIM)), _const_spec((1, D_RNN)), _const_spec((1, D_RNN))]


def _rglru_tile(x, gate, lru_w, xp_scr, h_scr):
    wc_ref, bc_ref, wa_ref, ba_ref, wx_ref, bx_ref, lam_ref = lru_w
    tm = x.shape[0]
    step = pl.program_id(0)

    ng = tm // SUBLANES
    x3 = x.reshape(ng, SUBLANES, D_RNN)
    sub = lax.broadcasted_iota(jnp.int32, (ng, SUBLANES, D_RNN), 1)
    prev_group = xp_scr[...].reshape(1, SUBLANES, D_RNN)
    xp_scr[...] = x3[ng - 1]

    wc = wc_ref[...]
    xc = bc_ref[...] + x3 * wc[CONV_W - 1:CONV_W]
    for k in range(1, CONV_W):
        rot = pltpu.roll(x3, k, 1)
        rot_prev = jnp.concatenate([pltpu.roll(prev_group, k, 1), rot[:ng - 1]], axis=0)
        xc = xc + jnp.where(sub >= k, rot, rot_prev) * wc[CONV_W - 1 - k:CONV_W - k]
    xc = xc.reshape(tm, D_RNN)

    a, mult, gated = _lru_coeffs(xc, wa_ref, ba_ref, wx_ref, bx_ref, lam_ref)
    rows = lax.broadcasted_iota(jnp.int32, (tm, D_RNN), 0)
    mult = jnp.where(rows + step * tm == 0, 1.0, mult)
    a = a.reshape(ng, SUBLANES, D_RNN)
    b = (mult * gated).reshape(ng, SUBLANES, D_RNN)

    d = 1
    while d < SUBLANES:
        keep = sub >= d
        b = jnp.where(keep, a * pltpu.roll(b, d, 1) + b, b)
        a = jnp.where(keep, a * pltpu.roll(a, d, 1), a)
        d *= 2
    h_prev = h_scr[...]
    hs = []
    for g in range(ng):
        hg = a[g] * h_prev + b[g]
        hs.append(hg)
        h_prev = hg[SUBLANES - 1:SUBLANES, :]
    h_scr[...] = h_prev
    return jax.nn.sigmoid(gate) * jnp.concatenate(hs, axis=0), h_prev


def _rglru_sample_kernel(xr_ref, g_ref, s0_ref, s1_ref, s2_ref, h0_ref, wc_ref, bc_ref, wa_ref, ba_ref,
                         wx_ref, bx_ref, lam_ref, yg_ref, h_ref, *, at_start):
    wc = wc_ref[...]
    xc = (bc_ref[...] + xr_ref[...] * wc[3:4] + s2_ref[...] * wc[2:3] + s1_ref[...] * wc[1:2]
          + s0_ref[...] * wc[0:1])
    a, mult, gated = _lru_coeffs(xc, wa_ref, ba_ref, wx_ref, bx_ref, lam_ref)
    if at_start:
        mult = jnp.ones_like(mult)
    h = a * h0_ref[...] + mult * gated
    h_ref[...] = h
    yg_ref[...] = jax.nn.sigmoid(g_ref[...]) * h


def _rglru_sample(xr, g_rnn, conv_rows, h0, w, at_start):
    m = xr.shape[0]
    full = _const_spec((m, D_RNN))
    return pl.pallas_call(
        functools.partial(_rglru_sample_kernel, at_start=at_start), grid=(1,), in_specs=[full] * 6 + _lru_specs(),
        out_specs=[full, full],
        out_shape=[jax.ShapeDtypeStruct((m, D_MODEL), F32), jax.ShapeDtypeStruct((m, D_RNN), F32)],
        compiler_params=_params(), name="rglru_sample",
    )(xr, g_rnn, *conv_rows, h0, *_lru_operands(w))


def _value_up_and_merge(o_of_head, ga_ref, yg_ref, wuv_ref, out_ref):
    for hd in range(N_HEADS):
        cols = slice(hd * V_DIM, (hd + 1) * V_DIM)
        y = _dot(o_of_head(hd).astype(BF16), wuv_ref[hd])
        out_ref[:, cols] = (jax.nn.sigmoid(ga_ref[:, cols]) * y + yg_ref[:, cols]).astype(out_ref.dtype)


def _attn_prompt_kernel(qlat_ref, qpe_ref, qlat_nx_ref, qpe_nx_ref, latT_ref, kpeT_ref, lat_ref, ga_ref, yg_ref,
                        wuv_ref, out_ref, s0_scr, s1_scr, s2_scr, m_scr, l_scr, acc_scr, *, tq, tk, tc, hg):
    qi = pl.program_id(0)
    rows = N_HEADS * tq
    grp = hg * tq
    n_sub = tk // tc
    m_scr[...] = jnp.full((rows, LANES), NEG, F32)
    l_scr[...] = jnp.zeros((rows, LANES), F32)
    acc_scr[...] = jnp.zeros((rows, KV_LORA), F32)

    groups = range(N_HEADS // hg)
    this_q, next_q = (qlat_ref, qpe_ref), (qlat_nx_ref, qpe_nx_ref)

    def scores_into(s_scr, j, g, q_refs):
        q_lat = q_refs[0][g * hg:(g + 1) * hg].reshape(grp, KV_LORA)
        q_pe = q_refs[1][g * hg:(g + 1) * hg].reshape(grp, QK_ROPE)
        for c in range(n_sub):
            s_scr[g * grp:(g + 1) * grp, c * tc:(c + 1) * tc] = (
                _dot(q_lat, latT_ref[j * n_sub + c]) + _dot(q_pe, kpeT_ref[j * n_sub + c]))

    def process(s_scr, j, g, masked):
        start = pl.multiple_of(j * tk, tk)
        r = slice(g * grp, (g + 1) * grp)
        s = s_scr[r, :]
        if masked:
            qpos = qi * tq + lax.broadcasted_iota(jnp.int32, (grp, tk), 0) % tq
            kpos = start + lax.broadcasted_iota(jnp.int32, (grp, tk), 1)
            s = jnp.where(kpos <= qpos, s, NEG)
        m_prev = m_scr[r, :]
        m_new = jnp.maximum(m_prev, jnp.max(s, axis=1, keepdims=True))
        alpha = jnp.exp2(m_prev - m_new)
        p = jnp.exp2(s - jnp.concatenate([m_new] * (tk // LANES), axis=1))
        lane_sums = p[:, :LANES]
        for c in range(1, tk // LANES):
            lane_sums = lane_sums + p[:, c * LANES:(c + 1) * LANES]
        l_scr[r, :] = alpha * l_scr[r, :] + lane_sums
        acc_scr[r, :] = (jnp.concatenate([alpha] * (KV_LORA // LANES), axis=1) * acc_scr[r, :]
                         + _dot(p.astype(BF16), lat_ref[pl.ds(start, tk), :]))
        m_scr[r, :] = m_new

    def step(cur_scr, j, nxt, masked=False, scores_first=True):
        for g in groups:
            if scores_first:
                scores_into(nxt[0], nxt[1], g, nxt[2])
            process(cur_scr, j, g, masked)
            if not scores_first:
                scores_into(nxt[0], nxt[1], g, nxt[2])

    n_full = (qi * tq) // tk
    last_kblk = lat_ref.shape[0] // tk - 1
    handoff = (s2_scr, jnp.minimum(((qi + 1) * tq) // tk, last_kblk), next_q)

    @pl.when(qi == 0)
    def _():
        for g in groups:
            scores_into(s2_scr, 0, g, this_q)

    @pl.when(n_full == 0)
    def _():
        step(s2_scr, n_full, handoff, masked=True, scores_first=False)

    @pl.when(n_full > 0)
    def _():
        step(s2_scr, n_full, (s0_scr, 0, this_q), masked=True)

    pairs = jnp.maximum(n_full - 1, 0) // 2
    rest = n_full - 2 * pairs

    def body(t, carry):
        step(s0_scr, 2 * t, (s1_scr, 2 * t + 1, this_q))
        step(s1_scr, 2 * t + 1, (s0_scr, 2 * t + 2, this_q))
        return carry

    lax.fori_loop(0, pairs, body, 0)

    @pl.when(rest == 1)
    def _():
        step(s0_scr, 2 * pairs, handoff)

    @pl.when(rest == 2)
    def _():
        step(s0_scr, 2 * pairs, (s1_scr, 2 * pairs + 1, this_q))
        step(s1_scr, 2 * pairs + 1, handoff)

    acc_scr[...] = acc_scr[...] / jnp.sum(l_scr[...], axis=1, keepdims=True)
    _value_up_and_merge(lambda hd: acc_scr[hd * tq:(hd + 1) * tq, :], ga_ref, yg_ref, wuv_ref, out_ref)


def _attn_prompt(q_lat, q_pe, lat_t, kpe_t, lat_b, g_att, yg, w, tq, tk, hg):
    s_len = lat_b.shape[0]
    n_col, _, tc = lat_t.shape
    row = lambda i: (i, 0)
    n_q = s_len // tq
    nxt = lambda i: (0, jnp.minimum(i + 1, n_q - 1), 0)
    in_specs = [pl.BlockSpec((N_HEADS, tq, KV_LORA), lambda i: (0, i, 0)),
                pl.BlockSpec((N_HEADS, tq, QK_ROPE), lambda i: (0, i, 0)),
                pl.BlockSpec((N_HEADS, tq, KV_LORA), nxt), pl.BlockSpec((N_HEADS, tq, QK_ROPE), nxt),
                _const_spec((n_col, KV_LORA, tc)), _const_spec((n_col, QK_ROPE, tc)),
                _const_spec((s_len, KV_LORA)),
                pl.BlockSpec((tq, D_MODEL), row), pl.BlockSpec((tq, D_MODEL), row),
                _const_spec((N_HEADS, KV_LORA, V_DIM))]
    rows = N_HEADS * tq
    return pl.pallas_call(
        functools.partial(_attn_prompt_kernel, tq=tq, tk=tk, tc=tc, hg=hg), grid=(n_q,),
        in_specs=in_specs, out_specs=pl.BlockSpec((tq, D_MODEL), row),
        out_shape=jax.ShapeDtypeStruct((s_len, D_MODEL), BF16),
        scratch_shapes=[pltpu.VMEM((rows, tk), F32), pltpu.VMEM((rows, tk), F32), pltpu.VMEM((rows, tk), F32),
                        pltpu.VMEM((rows, LANES), F32), pltpu.VMEM((rows, LANES), F32),
                        pltpu.VMEM((rows, KV_LORA), F32)],
        compiler_params=_params(), name="attn_prompt",
    )(q_lat, q_pe, q_lat, q_pe, lat_t, kpe_t, lat_b, g_att, yg, w["w_uv"])


def _attn_sample_kernel(pt_ref, qlat_ref, qpe_ref, latn_ref, kpen_ref, lat_hbm, kpt_hbm, o_ref,
                        lat_buf, kpt_buf, lat_sem, kpt_sem, m_scr, l_scr, acc_scr, *, pages, streams):
    b, c = pl.program_id(0), pl.program_id(1)
    n_c = pl.num_programs(1)
    last_step = pl.num_programs(0) * n_c - 1
    step = b * n_c + c
    q_lat = qlat_ref[:, 0, :]
    q_pe = qpe_ref[:, 0, :]
    per = pages // streams

    def page_copies(row, chunk, slot, i):
        page = pt_ref[row, chunk * pages + i]
        return (pltpu.make_async_copy(lat_hbm.at[0, page], lat_buf.at[slot, i], lat_sem.at[slot, i]),
                pltpu.make_async_copy(kpt_hbm.at[0, page], kpt_buf.at[slot, i], kpt_sem.at[slot, i]))

    def start_chunk(row, chunk, slot, i):
        for cp in page_copies(row, chunk, slot, i):
            cp.start()

    def wait_chunk(row, chunk, slot):
        for i in range(pages):
            for cp in page_copies(row, chunk, slot, i):
                cp.wait()

    def consume(slot, prefetch):
        kls, s_parts = [], []
        for i in range(pages):
            prefetch(i)
            kl = lat_buf[slot, i].astype(BF16)
            kls.append(kl)
            s_parts.append(_dot_nt(q_lat, kl) + _dot(q_pe, kpt_buf[slot, i].astype(BF16)))
        probs, alphas = [], []
        for st in range(streams):
            s = jnp.concatenate(s_parts[st * per:(st + 1) * per], axis=1)
            m_prev = m_scr[st]
            m_new = jnp.maximum(m_prev, jnp.max(s, axis=1, keepdims=True))
            alpha = jnp.exp2(m_prev - m_new)
            p = jnp.exp2(s - m_new)
            l_scr[st] = alpha * l_scr[st] + jnp.sum(p, axis=1, keepdims=True)
            m_scr[st] = m_new
            probs.append(p.astype(BF16))
            alphas.append(alpha)
        for st in range(streams):
            pv = _dot(probs[st][:, :PAGE_SIZE], kls[st * per])
            for i in range(1, per):
                pv += _dot(probs[st][:, i * PAGE_SIZE:(i + 1) * PAGE_SIZE], kls[st * per + i])
            acc_scr[st] = alphas[st] * acc_scr[st] + pv

    @pl.when(c == 0)
    def _():
        m_scr[...] = jnp.full((streams, N_HEADS, 1), NEG, F32)
        l_scr[...] = jnp.zeros((streams, N_HEADS, 1), F32)
        acc_scr[...] = jnp.zeros((streams, N_HEADS, KV_LORA), F32)

    cpr = n_c * DECODE_SLOTS
    last_chunk = pl.num_programs(0) * cpr - 1
    first = step * DECODE_SLOTS

    def locate(k):
        return k // cpr, k % cpr

    @pl.when(step == 0)
    def _():
        for k in range(DECODE_AHEAD):
            for i in range(pages):
                start_chunk(*locate(k), k, i)

    for j in range(DECODE_SLOTS):
        ahead = locate(jnp.minimum(first + j + DECODE_AHEAD, last_chunk))
        ahead_slot = (j + DECODE_AHEAD) % DECODE_SLOTS
        wait_chunk(*locate(first + j), j)
        consume(j, lambda i, ahead=ahead, ahead_slot=ahead_slot: start_chunk(*ahead, ahead_slot, i))

    @pl.when(step == last_step)
    def _():
        for j in range(DECODE_SLOTS - DECODE_AHEAD, DECODE_SLOTS):
            wait_chunk(*locate(last_chunk), (j + DECODE_AHEAD) % DECODE_SLOTS)

    @pl.when(c == n_c - 1)
    def _():
        lat_n = latn_ref[...]
        s_n = (jnp.sum(q_lat.astype(F32) * lat_n, axis=1, keepdims=True)
               + jnp.sum(q_pe.astype(F32) * kpen_ref[...], axis=1, keepdims=True))
        m_fin = s_n
        for st in range(streams):
            m_fin = jnp.maximum(m_fin, m_scr[st])
        p_n = jnp.exp2(s_n - m_fin)
        num, den = p_n * lat_n, p_n
        for st in range(streams):
            corr = jnp.exp2(m_scr[st] - m_fin)
            num = num + acc_scr[st] * corr
            den = den + l_scr[st] * corr
        o_ref[:, 0, :] = num / den


def _attn_sample(q_lat, q_pe, lat_new, kpe_new, pool_lat, pool_kpe_t, page_table, pages, streams):
    bd, n_pages = page_table.shape
    assert n_pages % (DECODE_SLOTS * pages) == 0, "each grid step consumes DECODE_SLOTS chunks of pages"
    per_b = lambda b, c, pt: (0, b, 0, 0)
    in_specs = [pl.BlockSpec((N_HEADS, None, 1, KV_LORA), per_b), pl.BlockSpec((N_HEADS, None, 1, QK_ROPE), per_b),
                pl.BlockSpec((None, 1, KV_LORA), lambda b, c, pt: (b, 0, 0)),
                pl.BlockSpec((None, 1, QK_ROPE), lambda b, c, pt: (b, 0, 0)),
                pl.BlockSpec(memory_space=pl.ANY), pl.BlockSpec(memory_space=pl.ANY)]
    grid_spec = pltpu.PrefetchScalarGridSpec(
        num_scalar_prefetch=1, grid=(bd, n_pages // (DECODE_SLOTS * pages)), in_specs=in_specs,
        out_specs=pl.BlockSpec((N_HEADS, None, 1, KV_LORA), per_b),
        scratch_shapes=[pltpu.VMEM((DECODE_SLOTS, pages, PAGE_SIZE, KV_LORA), F32),
                        pltpu.VMEM((DECODE_SLOTS, pages, QK_ROPE, PAGE_SIZE), F32),
                        pltpu.SemaphoreType.DMA((DECODE_SLOTS, pages)),
                        pltpu.SemaphoreType.DMA((DECODE_SLOTS, pages)),
                        pltpu.VMEM((streams, N_HEADS, 1), F32), pltpu.VMEM((streams, N_HEADS, 1), F32),
                        pltpu.VMEM((streams, N_HEADS, KV_LORA), F32)])
    o = pl.pallas_call(
        functools.partial(_attn_sample_kernel, pages=pages, streams=streams), grid_spec=grid_spec,
        out_shape=jax.ShapeDtypeStruct((N_HEADS, bd, 1, KV_LORA), F32),
        compiler_params=_params(2), name="attn_sample",
    )(page_table, q_lat.reshape(N_HEADS, bd, 1, KV_LORA), q_pe.reshape(N_HEADS, bd, 1, QK_ROPE),
      lat_new.reshape(bd, 1, KV_LORA), kpe_new.reshape(bd, 1, QK_ROPE), pool_lat, pool_kpe_t)
    return o.reshape(N_HEADS, bd, KV_LORA)


def _merge_sample_kernel(o_ref, ga_ref, yg_ref, wuv_ref, out_ref):
    _value_up_and_merge(lambda hd: o_ref[hd], ga_ref, yg_ref, wuv_ref, out_ref)


def _merge_sample(o, g_att, yg, w):
    m = g_att.shape[0]
    return pl.pallas_call(
        _merge_sample_kernel, grid=(1,),
        in_specs=[_const_spec((N_HEADS, m, KV_LORA)), _const_spec((m, D_MODEL)), _const_spec((m, D_MODEL)),
                  _const_spec((N_HEADS, KV_LORA, V_DIM))],
        out_specs=_const_spec((m, D_MODEL)),
        out_shape=jax.ShapeDtypeStruct((m, D_MODEL), BF16),
        compiler_params=_params(), name="merge_sample",
    )(o, g_att, yg, w["w_uv"])


def _outmlp_kernel(mg_ref, x_ref, gt1_ref, sh2_ref, sc2_ref, gt2_ref, gpm_ref, gpre_ref, gpost_ref,
                   wo_ref, wup_ref, wdn_ref, y_ref, *, ff_chunk):
    x1 = x_ref[...] + gt1_ref[...] * _rms(_dot(mg_ref[...], wo_ref[...]), gpm_ref[...])
    h2 = (_rms(x1, gpre_ref[...]) * (1.0 + sc2_ref[...]) + sh2_ref[...]).astype(BF16)
    f = None
    for j in range(D_FF // ff_chunk):
        cols = slice(j * ff_chunk, (j + 1) * ff_chunk)
        u = jnp.maximum(_dot(h2, wup_ref[:, cols]), 0.0)
        part = _dot((u * u).astype(BF16), wdn_ref[cols, :])
        f = part if f is None else f + part
    y_ref[...] = x1 + gt2_ref[...] * _rms(f, gpost_ref[...])


def _outmlp(merged, x, gt1, sh2, sc2, gt2, w, tm, ff_chunk):
    m = x.shape[0]
    per_row = gt1.shape[0] != 1
    row = lambda i: (i, 0)
    mod_spec = pl.BlockSpec((tm, D_MODEL), row) if per_row else _const_spec((1, D_MODEL))
    vec = _const_spec((1, D_MODEL))
    in_specs = [pl.BlockSpec((tm, D_MODEL), row), pl.BlockSpec((tm, D_MODEL), row),
                mod_spec, mod_spec, mod_spec, mod_spec, vec, vec, vec,
                _const_spec((D_MODEL, D_MODEL)), _const_spec((D_MODEL, D_FF)), _const_spec((D_FF, D_MODEL))]
    return pl.pallas_call(
        functools.partial(_outmlp_kernel, ff_chunk=ff_chunk), grid=(m // tm,), in_specs=in_specs,
        out_specs=pl.BlockSpec((tm, D_MODEL), row),
        out_shape=jax.ShapeDtypeStruct((m, D_MODEL), F32),
        compiler_params=_params(), name="outmlp",
    )(merged, x, gt1, sh2, sc2, gt2, w["g_post_mix"], w["g_pre_mlp"], w["g_post_mlp"],
      w["w_o"], w["w_up"], w["w_down"])


def _rope_tables(pos):
    half = QK_ROPE // 2
    inv = 1.0 / (ROPE_THETA ** (jnp.arange(half, dtype=F32) / half))
    ang = pos.astype(F32)[:, None] * jnp.tile(inv, 2 * LANES // QK_ROPE)[None, :]
    sign = jnp.tile(jnp.concatenate([-jnp.ones((half,), F32), jnp.ones((half,), F32)]), LANES // QK_ROPE)
    return jnp.cos(ang), jnp.sin(ang) * sign[None, :]


def _block_diag_tiles(wg):
    per = MXU_DIM // RNN_BW
    wg = wg.reshape(D_RNN // MXU_DIM, per, RNN_BW, RNN_BW)
    eye = jnp.eye(per, dtype=wg.dtype)
    return jnp.einsum("tpij,pq->tpiqj", wg, eye).reshape(D_RNN // MXU_DIM, MXU_DIM, MXU_DIM).astype(BF16)


def _layer_weights(l, w_in, w_conv, b_conv, w_gate_a, b_gate_a, w_gate_x, b_gate_x, lru_lambda, g_q, w_uq,
                   g_kv, w_uk, w_uv, w_o, w_up, w_down, g_pre_mix, g_post_mix, g_pre_mlp, g_post_mlp):
    wi = w_in[l]
    s_ckv = D_RNN + Q_LORA + KV_LORA
    s_kpe = s_ckv + QK_ROPE
    w_in_r = jnp.concatenate([wi[:, :s_ckv], wi[:, s_kpe:], wi[:, s_ckv:s_kpe],
                              jnp.zeros((D_MODEL, LANES - QK_ROPE), wi.dtype)], axis=1).astype(BF16)
    uq = w_uq[l]
    w_uq_r = jnp.concatenate([uq[:, :, :QK_NOPE].reshape(Q_LORA, N_HEADS * QK_NOPE),
                              uq[:, :, QK_NOPE:].reshape(Q_LORA, N_HEADS * QK_ROPE)], axis=1).astype(BF16)
    vec = lambda v: v[l].reshape(1, -1)
    return {
        "w_in": w_in_r, "w_uq": w_uq_r,
        "w_uk": jnp.transpose(w_uk[l], (1, 2, 0)).astype(BF16),
        "w_uv": jnp.transpose(w_uv[l], (1, 0, 2)).astype(BF16),
        "w_o": w_o[l].astype(BF16), "w_up": w_up[l].astype(BF16), "w_down": w_down[l].astype(BF16),
        "w_gate_a": _block_diag_tiles(w_gate_a[l]), "w_gate_x": _block_diag_tiles(w_gate_x[l]),
        "w_conv": w_conv[l], "b_conv": vec(b_conv), "b_gate_a": vec(b_gate_a), "b_gate_x": vec(b_gate_x),
        "lru_lambda": vec(lru_lambda), "g_q": vec(g_q), "g_kv": vec(g_kv), "g_pre_mix": vec(g_pre_mix),
        "g_post_mix": vec(g_post_mix), "g_pre_mlp": vec(g_pre_mlp), "g_post_mlp": vec(g_post_mlp),
    }


def _tile(n, pref):
    return pref if n % pref == 0 else n


def kernel(x_prompt, x_sample, c_prompt, c_sample, cache_latent, cache_krope, state_conv, state_rnn, page_table,
           w_ada, b_ada, g_pre_mix, g_post_mix, g_pre_mlp, g_post_mlp, w_in, w_conv, b_conv, w_gate_a, b_gate_a,
           w_gate_x, b_gate_x, lru_lambda, g_q, w_uq, g_kv, w_uk, w_uv, w_o, w_up, w_down):
    bp, s_len, _ = x_prompt.shape
    bd, t_dec, _ = x_sample.shape
    depth = w_in.shape[0]
    assert bp == 1 and t_dec == 1, "one prompt sequence and one new token per decode row"
    n_pages = page_table.shape[1]
    past_len = n_pages * PAGE_SIZE
    pages = next(p for p in (16, 8, 4, 2, 1) if n_pages % (DECODE_SLOTS * p) == 0)
    streams = min(pages, 4)

    xp = x_prompt.reshape(s_len, D_MODEL)
    xs = x_sample.reshape(bd, D_MODEL)
    c_all = jnp.concatenate([c_prompt, jnp.zeros((7, D_MODEL), F32), c_sample], axis=0)
    tabs_p = _rope_tables(jnp.arange(s_len))
    tabs_s = _rope_tables(jnp.full((bd,), past_len))
    tm_in, tm_out = _tile(s_len, 256), _tile(s_len, 512)
    tq, tk, hg = _tile(s_len, 128), _tile(s_len, 512), 2

    outs = [[] for _ in range(8)]
    for l in range(depth):
        w = _layer_weights(l, w_in, w_conv, b_conv, w_gate_a, b_gate_a, w_gate_x, b_gate_x, lru_lambda, g_q,
                           w_uq, g_kv, w_uk, w_uv, w_o, w_up, w_down, g_pre_mix, g_post_mix, g_pre_mlp,
                           g_post_mlp)
        mod = _ada(c_all, w_ada[l].astype(BF16), b_ada[l].reshape(1, -1))
        mod_p = [mod[0:1, k * D_MODEL:(k + 1) * D_MODEL] for k in range(6)]
        mod_s = [mod[8:, k * D_MODEL:(k + 1) * D_MODEL] for k in range(6)]

        yg, h_last, x_tail, g_att, q_lat, q_pe, lat, lat_b, kpe, lat_t, kpe_t = _inproj(
            xp, mod_p[0], mod_p[1], w, tabs_p, tm_in, scan=True)
        merged = _attn_prompt(q_lat, q_pe, lat_t, kpe_t, lat_b, g_att, yg, w, tq, tk, hg)
        xp = _outmlp(merged, xp, mod_p[2], mod_p[3], mod_p[4], mod_p[5], w, tm_out, 1024)
        outs[0].append(lat.reshape(1, s_len, KV_LORA))
        outs[1].append(kpe.reshape(1, s_len, QK_ROPE))
        outs[2].append(x_tail[SUBLANES - (CONV_W - 1):].reshape(1, CONV_W - 1, D_RNN))
        outs[3].append(h_last)

        xr, g_rnn, g_att, q_lat, q_pe, lat, _, kpe, _, _ = _inproj(
            xs, mod_s[0], mod_s[1], w, tabs_s, bd, scan=False)
        conv_rows = [state_conv[l][:, k, :] for k in range(CONV_W - 1)]
        yg, h_new = _rglru_sample(xr, g_rnn, conv_rows, state_rnn[l], w, at_start=(past_len == 0))
        pool_kpe_t = jnp.swapaxes(cache_krope[l:l + 1], 2, 3)
        o = _attn_sample(q_lat, q_pe, lat, kpe, cache_latent[l:l + 1], pool_kpe_t, page_table, pages, streams)
        merged = _merge_sample(o, g_att, yg, w)
        xs = _outmlp(merged, xs, mod_s[2], mod_s[3], mod_s[4], mod_s[5], w, bd, 1024)
        outs[4].append(lat.reshape(bd, 1, KV_LORA))
        outs[5].append(kpe.reshape(bd, 1, QK_ROPE))
        outs[6].append(jnp.stack(conv_rows[1:] + [xr], axis=1))
        outs[7].append(h_new)

    return (xp.reshape(1, s_len, D_MODEL), xs.reshape(bd, 1, D_MODEL)) + tuple(jnp.stack(o) for o in outs)
```

```python
import functools

import jax
import jax.numpy as jnp
from jax import lax
from jax.experimental import pallas as pl
from jax.experimental.pallas import tpu as pltpu

D_MODEL = 1024
D_RNN = 1024
RNN_BLOCKS = 16
RNN_BW = D_RNN // RNN_BLOCKS
CONV_W = 4
LRU_C = 8.0
N_HEADS = 8
QK_NOPE = 128
QK_ROPE = 64
V_DIM = D_MODEL // N_HEADS
Q_LORA = 384
KV_LORA = 256
ROPE_THETA = 10000.0
SM_SCALE = (QK_NOPE + QK_ROPE) ** -0.5
Q_SCALE = SM_SCALE * 1.4426950408889634
D_FF = 4 * D_MODEL
EPS = 1e-6
NEG = -1e30
PAGE_SIZE = 128

LANES = 128
SUBLANES = 8
DECODE_SLOTS = 4
DECODE_AHEAD = DECODE_SLOTS - 1
MXU_DIM = 256
VMEM_LIMIT = 56 * 1024 * 1024

Z_XR = 0
Z_CQ = Z_XR + D_RNN
Z_CKV = Z_CQ + Q_LORA
Z_GR = Z_CKV + KV_LORA
Z_GA = Z_GR + D_MODEL
Z_KPE = Z_GA + D_MODEL
Z_END = Z_KPE + LANES

BF16 = jnp.bfloat16
F32 = jnp.float32


def _rms(x, g):
    return x * lax.rsqrt(jnp.mean(x * x, axis=-1, keepdims=True) + EPS) * g


def _dot(a, b):
    return jnp.dot(a, b, preferred_element_type=F32)


def _dot_nt(a, b):
    return lax.dot_general(a, b, (((1,), (1,)), ((), ())), preferred_element_type=F32)


def _const_spec(shape):
    zeros = (0,) * len(shape)
    return pl.BlockSpec(shape, lambda *_: zeros, pipeline_mode=pl.Buffered(1))


def _params(n_axes=1):
    return pltpu.CompilerParams(dimension_semantics=("arbitrary",) * n_axes, vmem_limit_bytes=VMEM_LIMIT)


def _ada_kernel(c_ref, w_ref, b_ref, o_ref):
    c = c_ref[...]
    o_ref[...] = _dot((c * jax.nn.sigmoid(c)).astype(BF16), w_ref[...]) + b_ref[...]


def _ada(c_all, w_ada, b_ada):
    m = c_all.shape[0]
    tn = D_MODEL
    return pl.pallas_call(
        _ada_kernel,
        grid=(6 * D_MODEL // tn,),
        in_specs=[pl.BlockSpec((m, D_MODEL), lambda j: (0, 0)),
                  pl.BlockSpec((D_MODEL, tn), lambda j: (0, j)),
                  pl.BlockSpec((1, tn), lambda j: (0, j))],
        out_specs=pl.BlockSpec((m, tn), lambda j: (0, j)),
        out_shape=jax.ShapeDtypeStruct((m, 6 * D_MODEL), F32),
        compiler_params=_params(),
        name="ada",
    )(c_all, w_ada, b_ada)


def _swap_halves_64(x):
    n = x.shape[-1]
    lane = lax.broadcasted_iota(jnp.int32, x.shape, x.ndim - 1)
    from_hi = pltpu.roll(x, n - QK_ROPE // 2, x.ndim - 1)
    from_lo = pltpu.roll(x, QK_ROPE // 2, x.ndim - 1)
    return jnp.where(lane % QK_ROPE < QK_ROPE // 2, from_hi, from_lo)


def _inproj_kernel(*refs, scan):
    (x_ref, sh_ref, sc_ref, gpre_ref, win_ref, gq_ref, wuq_ref, wuk_ref, gkv_ref, cos_ref, sin_ref) = refs[:11]
    if scan:
        lru_w = refs[11:18]
        (yg_ref, hlast_ref, xtail_ref, ga_ref, qlat_ref, qpe_ref, lat_ref, latb_ref, kpe_ref, latT_ref, kpeT_ref,
         xp_scr, h_scr) = refs[18:]
    else:
        (xr_ref, gr_ref, ga_ref, qlat_ref, qpe_ref, lat_ref, latb_ref, kpe_ref, latT_ref, kpeT_ref) = refs[11:]
    if scan:
        @pl.when(pl.program_id(0) == 0)
        def _():
            xp_scr[...] = jnp.zeros((SUBLANES, D_RNN), F32)
            h_scr[...] = jnp.zeros((1, D_RNN), F32)
    x = x_ref[...]
    h = _rms(x, gpre_ref[...]) * (1.0 + sc_ref[...]) + sh_ref[...]
    z = _dot(h.astype(BF16), win_ref[...])
    if scan:
        xr = z[:, Z_XR:Z_CQ]
        yg_ref[...], hlast_ref[...] = _rglru_tile(xr, z[:, Z_GR:Z_GA], lru_w, xp_scr, h_scr)
        xtail_ref[...] = xr[xr.shape[0] - SUBLANES:, :]
    else:
        xr_ref[...] = z[:, Z_XR:Z_CQ]
        gr_ref[...] = z[:, Z_GR:Z_GA]
    ga_ref[...] = z[:, Z_GA:Z_KPE]

    q = _dot(_rms(z[:, Z_CQ:Z_CKV], gq_ref[...]).astype(BF16), wuq_ref[...])
    q_rope = q[:, N_HEADS * QK_NOPE:]
    cos, sin = cos_ref[...], sin_ref[...]
    reps = N_HEADS * QK_ROPE // LANES
    q_rope = (q_rope * jnp.concatenate([cos] * reps, axis=1)
              + _swap_halves_64(q_rope) * jnp.concatenate([sin] * reps, axis=1)) * Q_SCALE
    for hd in range(N_HEADS):
        q_nope = q[:, hd * QK_NOPE:(hd + 1) * QK_NOPE].astype(BF16)
        qlat_ref[hd] = (_dot(q_nope, wuk_ref[hd]) * Q_SCALE).astype(BF16)
        qpe_ref[hd] = q_rope[:, hd * QK_ROPE:(hd + 1) * QK_ROPE].astype(BF16)

    lat = _rms(z[:, Z_CKV:Z_GR], gkv_ref[...])
    lat_ref[...] = lat
    latb_ref[...] = lat.astype(BF16)
    latT_ref[0] = lat.T.astype(BF16)
    k_raw = z[:, Z_KPE:Z_END]
    kpe = k_raw * cos + _swap_halves_64(k_raw) * sin
    kpe_ref[...] = kpe[:, :QK_ROPE]
    kpeT_ref[0] = kpe.T[:QK_ROPE, :].astype(BF16)


def _inproj(x, sh, sc, w, rope_tabs, tm, scan):
    m = x.shape[0]
    per_row = sh.shape[0] != 1
    row = lambda i: (i, 0)
    fixed = lambda i: (0, 0)
    mod_spec = pl.BlockSpec((tm, D_MODEL), row) if per_row else _const_spec((1, D_MODEL))
    cos, sin = rope_tabs
    operands = [x, sh, sc, w["g_pre_mix"], w["w_in"], w["g_q"], w["w_uq"], w["w_uk"], w["g_kv"], cos, sin]
    in_specs = [
        pl.BlockSpec((tm, D_MODEL), row), mod_spec, mod_spec,
        _const_spec((1, D_MODEL)), _const_spec((D_MODEL, Z_END)), _const_spec((1, Q_LORA)),
        _const_spec((Q_LORA, N_HEADS * (QK_NOPE + QK_ROPE))), _const_spec((N_HEADS, QK_NOPE, KV_LORA)),
        _const_spec((1, KV_LORA)),
        pl.BlockSpec((tm, LANES), row), pl.BlockSpec((tm, LANES), row),
    ]
    if scan:
        operands += _lru_operands(w)
        in_specs += _lru_specs()
        out_shape = [jax.ShapeDtypeStruct((m, D_MODEL), F32), jax.ShapeDtypeStruct((1, D_RNN), F32),
                     jax.ShapeDtypeStruct((SUBLANES, D_RNN), F32)]
        out_specs = [pl.BlockSpec((tm, D_MODEL), row), pl.BlockSpec((1, D_RNN), fixed),
                     pl.BlockSpec((SUBLANES, D_RNN), fixed)]
        scratch = [pltpu.VMEM((SUBLANES, D_RNN), F32), pltpu.VMEM((1, D_RNN), F32)]
    else:
        out_shape = [jax.ShapeDtypeStruct((m, D_RNN), F32), jax.ShapeDtypeStruct((m, D_MODEL), F32)]
        out_specs = [pl.BlockSpec((tm, D_RNN), row), pl.BlockSpec((tm, D_MODEL), row)]
        scratch = []
    out_shape += [
        jax.ShapeDtypeStruct((m, D_MODEL), F32),
        jax.ShapeDtypeStruct((N_HEADS, m, KV_LORA), BF16), jax.ShapeDtypeStruct((N_HEADS, m, QK_ROPE), BF16),
        jax.ShapeDtypeStruct((m, KV_LORA), F32), jax.ShapeDtypeStruct((m, KV_LORA), BF16),
        jax.ShapeDtypeStruct((m, QK_ROPE), F32),
        jax.ShapeDtypeStruct((m // tm, KV_LORA, tm), BF16), jax.ShapeDtypeStruct((m // tm, QK_ROPE, tm), BF16),
    ]
    out_specs += [
        pl.BlockSpec((tm, D_MODEL), row),
        pl.BlockSpec((N_HEADS, tm, KV_LORA), lambda i: (0, i, 0)),
        pl.BlockSpec((N_HEADS, tm, QK_ROPE), lambda i: (0, i, 0)),
        pl.BlockSpec((tm, KV_LORA), row), pl.BlockSpec((tm, KV_LORA), row),
        pl.BlockSpec((tm, QK_ROPE), row),
        pl.BlockSpec((1, KV_LORA, tm), lambda i: (i, 0, 0)), pl.BlockSpec((1, QK_ROPE, tm), lambda i: (i, 0, 0)),
    ]
    return pl.pallas_call(
        functools.partial(_inproj_kernel, scan=scan), grid=(m // tm,), in_specs=in_specs, out_specs=out_specs,
        out_shape=out_shape, scratch_shapes=scratch, compiler_params=_params(), name="inproj",
    )(*operands)


def _lru_coeffs(xc, wa_ref, ba_ref, wx_ref, bx_ref, lam_ref):
    xb = xc.astype(BF16)
    r_parts, i_parts = [], []
    for j in range(D_RNN // MXU_DIM):
        blk = xb[:, j * MXU_DIM:(j + 1) * MXU_DIM]
        r_parts.append(_dot(blk, wa_ref[j]))
        i_parts.append(_dot(blk, wx_ref[j]))
    r = jax.nn.sigmoid(jnp.concatenate(r_parts, axis=1) + ba_ref[...])
    i = jax.nn.sigmoid(jnp.concatenate(i_parts, axis=1) + bx_ref[...])
    lam = lam_ref[...]
    softplus_neg_lam = jnp.maximum(-lam, 0.0) + jnp.log1p(jnp.exp(-jnp.abs(lam)))
    log_a = -LRU_C * r * softplus_neg_lam
    a = jnp.exp(log_a)
    mult = jnp.sqrt(-jnp.tanh(log_a) * (a * a + 1.0))
    return a, mult, i * xc


def _lru_operands(w):
    return [w["w_conv"], w["b_conv"], w["w_gate_a"], w["b_gate_a"], w["w_gate_x"], w["b_gate_x"], w["lru_lambda"]]


def _lru_specs():
    nblk = D_RNN // MXU_DIM
    return [_const_spec((CONV_W, D_RNN)), _const_spec((1, D_RNN)),
            _const_spec((nblk, MXU_DIM, MXU_DIM)), _const_spec((1, D_RNN)),
            _const_spec((nblk, MXU_DIM, MXU_DIM)), _const_spec((1, D_RNN)), _const_spec((1, D_RNN))]


def _rglru_tile(x, gate, lru_w, xp_scr, h_scr):
    wc_ref, bc_ref, wa_ref, ba_ref, wx_ref, bx_ref, lam_ref = lru_w
    tm = x.shape[0]
    step = pl.program_id(0)

    ng = tm // SUBLANES
    x3 = x.reshape(ng, SUBLANES, D_RNN)
    sub = lax.broadcasted_iota(jnp.int32, (ng, SUBLANES, D_RNN), 1)
    prev_group = xp_scr[...].reshape(1, SUBLANES, D_RNN)
    xp_scr[...] = x3[ng - 1]

    wc = wc_ref[...]
    xc = bc_ref[...] + x3 * wc[CONV_W - 1:CONV_W]
    for k in range(1, CONV_W):
        rot = pltpu.roll(x3, k, 1)
        rot_prev = jnp.concatenate([pltpu.roll(prev_group, k, 1), rot[:ng - 1]], axis=0)
        xc = xc + jnp.where(sub >= k, rot, rot_prev) * wc[CONV_W - 1 - k:CONV_W - k]
    xc = xc.reshape(tm, D_RNN)

    a, mult, gated = _lru_coeffs(xc, wa_ref, ba_ref, wx_ref, bx_ref, lam_ref)
    rows = lax.broadcasted_iota(jnp.int32, (tm, D_RNN), 0)
    mult = jnp.where(rows + step * tm == 0, 1.0, mult)
    a = a.reshape(ng, SUBLANES, D_RNN)
    b = (mult * gated).reshape(ng, SUBLANES, D_RNN)

    d = 1
    while d < SUBLANES:
        keep = sub >= d
        b = jnp.where(keep, a * pltpu.roll(b, d, 1) + b, b)
        a = jnp.where(keep, a * pltpu.roll(a, d, 1), a)
        d *= 2
    h_prev = h_scr[...]
    hs = []
    for g in range(ng):
        hg = a[g] * h_prev + b[g]
        hs.append(hg)
        h_prev = hg[SUBLANES - 1:SUBLANES, :]
    h_scr[...] = h_prev
    return jax.nn.sigmoid(gate) * jnp.concatenate(hs, axis=0), h_prev


def _rglru_sample_kernel(xr_ref, g_ref, s0_ref, s1_ref, s2_ref, h0_ref, wc_ref, bc_ref, wa_ref, ba_ref,
                         wx_ref, bx_ref, lam_ref, yg_ref, h_ref, *, at_start):
    wc = wc_ref[...]
    xc = (bc_ref[...] + xr_ref[...] * wc[3:4] + s2_ref[...] * wc[2:3] + s1_ref[...] * wc[1:2]
          + s0_ref[...] * wc[0:1])
    a, mult, gated = _lru_coeffs(xc, wa_ref, ba_ref, wx_ref, bx_ref, lam_ref)
    if at_start:
        mult = jnp.ones_like(mult)
    h = a * h0_ref[...] + mult * gated
    h_ref[...] = h
    yg_ref[...] = jax.nn.sigmoid(g_ref[...]) * h


def _rglru_sample(xr, g_rnn, conv_rows, h0, w, at_start):
    m = xr.shape[0]
    full = _const_spec((m, D_RNN))
    return pl.pallas_call(
        functools.partial(_rglru_sample_kernel, at_start=at_start), grid=(1,), in_specs=[full] * 6 + _lru_specs(),
        out_specs=[full, full],
        out_shape=[jax.ShapeDtypeStruct((m, D_MODEL), F32), jax.ShapeDtypeStruct((m, D_RNN), F32)],
        compiler_params=_params(), name="rglru_sample",
    )(xr, g_rnn, *conv_rows, h0, *_lru_operands(w))


def _value_up_and_merge(o_of_head, ga_ref, yg_ref, wuv_ref, out_ref):
    for hd in range(N_HEADS):
        cols = slice(hd * V_DIM, (hd + 1) * V_DIM)
        y = _dot(o_of_head(hd).astype(BF16), wuv_ref[hd])
        out_ref[:, cols] = (jax.nn.sigmoid(ga_ref[:, cols]) * y + yg_ref[:, cols]).astype(out_ref.dtype)


def _attn_prompt_kernel(qlat_ref, qpe_ref, qlat_nx_ref, qpe_nx_ref, latT_ref, kpeT_ref, lat_ref, ga_ref, yg_ref,
                        wuv_ref, out_ref, s0_scr, s1_scr, s2_scr, m_scr, l_scr, acc_scr, *, tq, tk, tc, hg):
    qi = pl.program_id(0)
    rows = N_HEADS * tq
    grp = hg * tq
    n_sub = tk // tc
    m_scr[...] = jnp.full((rows, LANES), NEG, F32)
    l_scr[...] = jnp.zeros((rows, LANES), F32)
    acc_scr[...] = jnp.zeros((rows, KV_LORA), F32)

    groups = range(N_HEADS // hg)
    this_q, next_q = (qlat_ref, qpe_ref), (qlat_nx_ref, qpe_nx_ref)

    def scores_into(s_scr, j, g, q_refs):
        q_lat = q_refs[0][g * hg:(g + 1) * hg].reshape(grp, KV_LORA)
        q_pe = q_refs[1][g * hg:(g + 1) * hg].reshape(grp, QK_ROPE)
        for c in range(n_sub):
            s_scr[g * grp:(g + 1) * grp, c * tc:(c + 1) * tc] = (
                _dot(q_lat, latT_ref[j * n_sub + c]) + _dot(q_pe, kpeT_ref[j * n_sub + c]))

    def process(s_scr, j, g, masked):
        start = pl.multiple_of(j * tk, tk)
        r = slice(g * grp, (g + 1) * grp)
        s = s_scr[r, :]
        if masked:
            qpos = qi * tq + lax.broadcasted_iota(jnp.int32, (grp, tk), 0) % tq
            kpos = start + lax.broadcasted_iota(jnp.int32, (grp, tk), 1)
            s = jnp.where(kpos <= qpos, s, NEG)
        m_prev = m_scr[r, :]
        m_new = jnp.maximum(m_prev, jnp.max(s, axis=1, keepdims=True))
        alpha = jnp.exp2(m_prev - m_new)
        p = jnp.exp2(s - jnp.concatenate([m_new] * (tk // LANES), axis=1))
        lane_sums = p[:, :LANES]
        for c in range(1, tk // LANES):
            lane_sums = lane_sums + p[:, c * LANES:(c + 1) * LANES]
        l_scr[r, :] = alpha * l_scr[r, :] + lane_sums
        acc_scr[r, :] = (jnp.concatenate([alpha] * (KV_LORA // LANES), axis=1) * acc_scr[r, :]
                         + _dot(p.astype(BF16), lat_ref[pl.ds(start, tk), :]))
        m_scr[r, :] = m_new

    def step(cur_scr, j, nxt, masked=False, scores_first=True):
        for g in groups:
            if scores_first:
                scores_into(nxt[0], nxt[1], g, nxt[2])
            process(cur_scr, j, g, masked)
            if not scores_first:
                scores_into(nxt[0], nxt[1], g, nxt[2])

    n_full = (qi * tq) // tk
    last_kblk = lat_ref.shape[0] // tk - 1
    handoff = (s2_scr, jnp.minimum(((qi + 1) * tq) // tk, last_kblk), next_q)

    @pl.when(qi == 0)
    def _():
        for g in groups:
            scores_into(s2_scr, 0, g, this_q)

    @pl.when(n_full == 0)
    def _():
        step(s2_scr, n_full, handoff, masked=True, scores_first=False)

    @pl.when(n_full > 0)
    def _():
        step(s2_scr, n_full, (s0_scr, 0, this_q), masked=True)

    pairs = jnp.maximum(n_full - 1, 0) // 2
    rest = n_full - 2 * pairs

    def body(t, carry):
        step(s0_scr, 2 * t, (s1_scr, 2 * t + 1, this_q))
        step(s1_scr, 2 * t + 1, (s0_scr, 2 * t + 2, this_q))
        return carry

    lax.fori_loop(0, pairs, body, 0)

    @pl.when(rest == 1)
    def _():
        step(s0_scr, 2 * pairs, handoff)

    @pl.when(rest == 2)
    def _():
        step(s0_scr, 2 * pairs, (s1_scr, 2 * pairs + 1, this_q))
        step(s1_scr, 2 * pairs + 1, handoff)

    acc_scr[...] = acc_scr[...] / jnp.sum(l_scr[...], axis=1, keepdims=True)
    _value_up_and_merge(lambda hd: acc_scr[hd * tq:(hd + 1) * tq, :], ga_ref, yg_ref, wuv_ref, out_ref)


def _attn_prompt(q_lat, q_pe, lat_t, kpe_t, lat_b, g_att, yg, w, tq, tk, hg):
    s_len = lat_b.shape[0]
    n_col, _, tc = lat_t.shape
    row = lambda i: (i, 0)
    n_q = s_len // tq
    nxt = lambda i: (0, jnp.minimum(i + 1, n_q - 1), 0)
    in_specs = [pl.BlockSpec((N_HEADS, tq, KV_LORA), lambda i: (0, i, 0)),
                pl.BlockSpec((N_HEADS, tq, QK_ROPE), lambda i: (0, i, 0)),
                pl.BlockSpec((N_HEADS, tq, KV_LORA), nxt), pl.BlockSpec((N_HEADS, tq, QK_ROPE), nxt),
                _const_spec((n_col, KV_LORA, tc)), _const_spec((n_col, QK_ROPE, tc)),
                _const_spec((s_len, KV_LORA)),
                pl.BlockSpec((tq, D_MODEL), row), pl.BlockSpec((tq, D_MODEL), row),
                _const_spec((N_HEADS, KV_LORA, V_DIM))]
    rows = N_HEADS * tq
    return pl.pallas_call(
        functools.partial(_attn_prompt_kernel, tq=tq, tk=tk, tc=tc, hg=hg), grid=(n_q,),
        in_specs=in_specs, out_specs=pl.BlockSpec((tq, D_MODEL), row),
        out_shape=jax.ShapeDtypeStruct((s_len, D_MODEL), BF16),
        scratch_shapes=[pltpu.VMEM((rows, tk), F32), pltpu.VMEM((rows, tk), F32), pltpu.VMEM((rows, tk), F32),
                        pltpu.VMEM((rows, LANES), F32), pltpu.VMEM((rows, LANES), F32),
                        pltpu.VMEM((rows, KV_LORA), F32)],
        compiler_params=_params(), name="attn_prompt",
    )(q_lat, q_pe, q_lat, q_pe, lat_t, kpe_t, lat_b, g_att, yg, w["w_uv"])


def _attn_sample_kernel(pt_ref, qlat_ref, qpe_ref, latn_ref, kpen_ref, lat_hbm, kpt_hbm, o_ref,
                        lat_buf, kpt_buf, lat_sem, kpt_sem, m_scr, l_scr, acc_scr, *, pages, streams):
    b, c = pl.program_id(0), pl.program_id(1)
    n_c = pl.num_programs(1)
    last_step = pl.num_programs(0) * n_c - 1
    step = b * n_c + c
    q_lat = qlat_ref[:, 0, :]
    q_pe = qpe_ref[:, 0, :]
    per = pages // streams

    def page_copies(row, chunk, slot, i):
        page = pt_ref[row, chunk * pages + i]
        return (pltpu.make_async_copy(lat_hbm.at[0, page], lat_buf.at[slot, i], lat_sem.at[slot, i]),
                pltpu.make_async_copy(kpt_hbm.at[0, page], kpt_buf.at[slot, i], kpt_sem.at[slot, i]))

    def start_chunk(row, chunk, slot, i):
        for cp in page_copies(row, chunk, slot, i):
            cp.start()

    def wait_chunk(row, chunk, slot):
        for i in range(pages):
            for cp in page_copies(row, chunk, slot, i):
                cp.wait()

    def consume(slot, prefetch):
        kls, s_parts = [], []
        for i in range(pages):
            prefetch(i)
            kl = lat_buf[slot, i].astype(BF16)
            kls.append(kl)
            s_parts.append(_dot_nt(q_lat, kl) + _dot(q_pe, kpt_buf[slot, i].astype(BF16)))
        probs, alphas = [], []
        for st in range(streams):
            s = jnp.concatenate(s_parts[st * per:(st + 1) * per], axis=1)
            m_prev = m_scr[st]
            m_new = jnp.maximum(m_prev, jnp.max(s, axis=1, keepdims=True))
            alpha = jnp.exp2(m_prev - m_new)
            p = jnp.exp2(s - m_new)
            l_scr[st] = alpha * l_scr[st] + jnp.sum(p, axis=1, keepdims=True)
            m_scr[st] = m_new
            probs.append(p.astype(BF16))
            alphas.append(alpha)
        for st in range(streams):
            pv = _dot(probs[st][:, :PAGE_SIZE], kls[st * per])
            for i in range(1, per):
                pv += _dot(probs[st][:, i * PAGE_SIZE:(i + 1) * PAGE_SIZE], kls[st * per + i])
            acc_scr[st] = alphas[st] * acc_scr[st] + pv

    @pl.when(c == 0)
    def _():
        m_scr[...] = jnp.full((streams, N_HEADS, 1), NEG, F32)
        l_scr[...] = jnp.zeros((streams, N_HEADS, 1), F32)
        acc_scr[...] = jnp.zeros((streams, N_HEADS, KV_LORA), F32)

    cpr = n_c * DECODE_SLOTS
    last_chunk = pl.num_programs(0) * cpr - 1
    first = step * DECODE_SLOTS

    def locate(k):
        return k // cpr, k % cpr

    @pl.when(step == 0)
    def _():
        for k in range(DECODE_AHEAD):
            for i in range(pages):
                start_chunk(*locate(k), k, i)

    for j in range(DECODE_SLOTS):
        ahead = locate(jnp.minimum(first + j + DECODE_AHEAD, last_chunk))
        ahead_slot = (j + DECODE_AHEAD) % DECODE_SLOTS
        wait_chunk(*locate(first + j), j)
        consume(j, lambda i, ahead=ahead, ahead_slot=ahead_slot: start_chunk(*ahead, ahead_slot, i))

    @pl.when(step == last_step)
    def _():
        for j in range(DECODE_SLOTS - DECODE_AHEAD, DECODE_SLOTS):
            wait_chunk(*locate(last_chunk), (j + DECODE_AHEAD) % DECODE_SLOTS)

    @pl.when(c == n_c - 1)
    def _():
        lat_n = latn_ref[...]
        s_n = (jnp.sum(q_lat.astype(F32) * lat_n, axis=1, keepdims=True)
               + jnp.sum(q_pe.astype(F32) * kpen_ref[...], axis=1, keepdims=True))
        m_fin = s_n
        for st in range(streams):
            m_fin = jnp.maximum(m_fin, m_scr[st])
        p_n = jnp.exp2(s_n - m_fin)
        num, den = p_n * lat_n, p_n
        for st in range(streams):
            corr = jnp.exp2(m_scr[st] - m_fin)
            num = num + acc_scr[st] * corr
            den = den + l_scr[st] * corr
        o_ref[:, 0, :] = num / den


def _attn_sample(q_lat, q_pe, lat_new, kpe_new, pool_lat, pool_kpe_t, page_table, pages, streams):
    bd, n_pages = page_table.shape
    assert n_pages % (DECODE_SLOTS * pages) == 0, "each grid step consumes DECODE_SLOTS chunks of pages"
    per_b = lambda b, c, pt: (0, b, 0, 0)
    in_specs = [pl.BlockSpec((N_HEADS, None, 1, KV_LORA), per_b), pl.BlockSpec((N_HEADS, None, 1, QK_ROPE), per_b),
                pl.BlockSpec((None, 1, KV_LORA), lambda b, c, pt: (b, 0, 0)),
                pl.BlockSpec((None, 1, QK_ROPE), lambda b, c, pt: (b, 0, 0)),
                pl.BlockSpec(memory_space=pl.ANY), pl.BlockSpec(memory_space=pl.ANY)]
    grid_spec = pltpu.PrefetchScalarGridSpec(
        num_scalar_prefetch=1, grid=(bd, n_pages // (DECODE_SLOTS * pages)), in_specs=in_specs,
        out_specs=pl.BlockSpec((N_HEADS, None, 1, KV_LORA), per_b),
        scratch_shapes=[pltpu.VMEM((DECODE_SLOTS, pages, PAGE_SIZE, KV_LORA), F32),
                        pltpu.VMEM((DECODE_SLOTS, pages, QK_ROPE, PAGE_SIZE), F32),
                        pltpu.SemaphoreType.DMA((DECODE_SLOTS, pages)),
                        pltpu.SemaphoreType.DMA((DECODE_SLOTS, pages)),
                        pltpu.VMEM((streams, N_HEADS, 1), F32), pltpu.VMEM((streams, N_HEADS, 1), F32),
                        pltpu.VMEM((streams, N_HEADS, KV_LORA), F32)])
    o = pl.pallas_call(
        functools.partial(_attn_sample_kernel, pages=pages, streams=streams), grid_spec=grid_spec,
        out_shape=jax.ShapeDtypeStruct((N_HEADS, bd, 1, KV_LORA), F32),
        compiler_params=_params(2), name="attn_sample",
    )(page_table, q_lat.reshape(N_HEADS, bd, 1, KV_LORA), q_pe.reshape(N_HEADS, bd, 1, QK_ROPE),
      lat_new.reshape(bd, 1, KV_LORA), kpe_new.reshape(bd, 1, QK_ROPE), pool_lat, pool_kpe_t)
    return o.reshape(N_HEADS, bd, KV_LORA)


def _merge_sample_kernel(o_ref, ga_ref, yg_ref, wuv_ref, out_ref):
    _value_up_and_merge(lambda hd: o_ref[hd], ga_ref, yg_ref, wuv_ref, out_ref)


def _merge_sample(o, g_att, yg, w):
    m = g_att.shape[0]
    return pl.pallas_call(
        _merge_sample_kernel, grid=(1,),
        in_specs=[_const_spec((N_HEADS, m, KV_LORA)), _const_spec((m, D_MODEL)), _const_spec((m, D_MODEL)),
                  _const_spec((N_HEADS, KV_LORA, V_DIM))],
        out_specs=_const_spec((m, D_MODEL)),
        out_shape=jax.ShapeDtypeStruct((m, D_MODEL), BF16),
        compiler_params=_params(), name="merge_sample",
    )(o, g_att, yg, w["w_uv"])


def _outmlp_kernel(mg_ref, x_ref, gt1_ref, sh2_ref, sc2_ref, gt2_ref, gpm_ref, gpre_ref, gpost_ref,
                   wo_ref, wup_ref, wdn_ref, y_ref, *, ff_chunk):
    x1 = x_ref[...] + gt1_ref[...] * _rms(_dot(mg_ref[...], wo_ref[...]), gpm_ref[...])
    h2 = (_rms(x1, gpre_ref[...]) * (1.0 + sc2_ref[...]) + sh2_ref[...]).astype(BF16)
    f = None
    for j in range(D_FF // ff_chunk):
        cols = slice(j * ff_chunk, (j + 1) * ff_chunk)
        u = jnp.maximum(_dot(h2, wup_ref[:, cols]), 0.0)
        part = _dot((u * u).astype(BF16), wdn_ref[cols, :])
        f = part if f is None else f + part
    y_ref[...] = x1 + gt2_ref[...] * _rms(f, gpost_ref[...])


def _outmlp(merged, x, gt1, sh2, sc2, gt2, w, tm, ff_chunk):
    m = x.shape[0]
    per_row = gt1.shape[0] != 1
    row = lambda i: (i, 0)
    mod_spec = pl.BlockSpec((tm, D_MODEL), row) if per_row else _const_spec((1, D_MODEL))
    vec = _const_spec((1, D_MODEL))
    in_specs = [pl.BlockSpec((tm, D_MODEL), row), pl.BlockSpec((tm, D_MODEL), row),
                mod_spec, mod_spec, mod_spec, mod_spec, vec, vec, vec,
                _const_spec((D_MODEL, D_MODEL)), _const_spec((D_MODEL, D_FF)), _const_spec((D_FF, D_MODEL))]
    return pl.pallas_call(
        functools.partial(_outmlp_kernel, ff_chunk=ff_chunk), grid=(m // tm,), in_specs=in_specs,
        out_specs=pl.BlockSpec((tm, D_MODEL), row),
        out_shape=jax.ShapeDtypeStruct((m, D_MODEL), F32),
        compiler_params=_params(), name="outmlp",
    )(merged, x, gt1, sh2, sc2, gt2, w["g_post_mix"], w["g_pre_mlp"], w["g_post_mlp"],
      w["w_o"], w["w_up"], w["w_down"])


def _rope_tables(pos):
    half = QK_ROPE // 2
    inv = 1.0 / (ROPE_THETA ** (jnp.arange(half, dtype=F32) / half))
    ang = pos.astype(F32)[:, None] * jnp.tile(inv, 2 * LANES // QK_ROPE)[None, :]
    sign = jnp.tile(jnp.concatenate([-jnp.ones((half,), F32), jnp.ones((half,), F32)]), LANES // QK_ROPE)
    return jnp.cos(ang), jnp.sin(ang) * sign[None, :]


def _block_diag_tiles(wg):
    per = MXU_DIM // RNN_BW
    wg = wg.reshape(D_RNN // MXU_DIM, per, RNN_BW, RNN_BW)
    eye = jnp.eye(per, dtype=wg.dtype)
    return jnp.einsum("tpij,pq->tpiqj", wg, eye).reshape(D_RNN // MXU_DIM, MXU_DIM, MXU_DIM).astype(BF16)


def _layer_weights(l, w_in, w_conv, b_conv, w_gate_a, b_gate_a, w_gate_x, b_gate_x, lru_lambda, g_q, w_uq,
                   g_kv, w_uk, w_uv, w_o, w_up, w_down, g_pre_mix, g_post_mix, g_pre_mlp, g_post_mlp):
    wi = w_in[l]
    s_ckv = D_RNN + Q_LORA + KV_LORA
    s_kpe = s_ckv + QK_ROPE
    w_in_r = jnp.concatenate([wi[:, :s_ckv], wi[:, s_kpe:], wi[:, s_ckv:s_kpe],
                              jnp.zeros((D_MODEL, LANES - QK_ROPE), wi.dtype)], axis=1).astype(BF16)
    uq = w_uq[l]
    w_uq_r = jnp.concatenate([uq[:, :, :QK_NOPE].reshape(Q_LORA, N_HEADS * QK_NOPE),
                              uq[:, :, QK_NOPE:].reshape(Q_LORA, N_HEADS * QK_ROPE)], axis=1).astype(BF16)
    vec = lambda v: v[l].reshape(1, -1)
    return {
        "w_in": w_in_r, "w_uq": w_uq_r,
        "w_uk": jnp.transpose(w_uk[l], (1, 2, 0)).astype(BF16),
        "w_uv": jnp.transpose(w_uv[l], (1, 0, 2)).astype(BF16),
        "w_o": w_o[l].astype(BF16), "w_up": w_up[l].astype(BF16), "w_down": w_down[l].astype(BF16),
        "w_gate_a": _block_diag_tiles(w_gate_a[l]), "w_gate_x": _block_diag_tiles(w_gate_x[l]),
        "w_conv": w_conv[l], "b_conv": vec(b_conv), "b_gate_a": vec(b_gate_a), "b_gate_x": vec(b_gate_x),
        "lru_lambda": vec(lru_lambda), "g_q": vec(g_q), "g_kv": vec(g_kv), "g_pre_mix": vec(g_pre_mix),
        "g_post_mix": vec(g_post_mix), "g_pre_mlp": vec(g_pre_mlp), "g_post_mlp": vec(g_post_mlp),
    }


def _tile(n, pref):
    return pref if n % pref == 0 else n


def kernel(x_prompt, x_sample, c_prompt, c_sample, cache_latent, cache_krope, state_conv, state_rnn, page_table,
           w_ada, b_ada, g_pre_mix, g_post_mix, g_pre_mlp, g_post_mlp, w_in, w_conv, b_conv, w_gate_a, b_gate_a,
           w_gate_x, b_gate_x, lru_lambda, g_q, w_uq, g_kv, w_uk, w_uv, w_o, w_up, w_down):
    bp, s_len, _ = x_prompt.shape
    bd, t_dec, _ = x_sample.shape
    depth = w_in.shape[0]
    assert bp == 1 and t_dec == 1, "one prompt sequence and one new token per decode row"
    n_pages = page_table.shape[1]
    past_len = n_pages * PAGE_SIZE
    pages = next(p for p in (16, 8, 4, 2, 1) if n_pages % (DECODE_SLOTS * p) == 0)
    streams = min(pages, 4)

    xp = x_prompt.reshape(s_len, D_MODEL)
    xs = x_sample.reshape(bd, D_MODEL)
    c_all = jnp.concatenate([c_prompt, jnp.zeros((7, D_MODEL), F32), c_sample], axis=0)
    tabs_p = _rope_tables(jnp.arange(s_len))
    tabs_s = _rope_tables(jnp.full((bd,), past_len))
    tm_in, tm_out = _tile(s_len, 256), _tile(s_len, 512)
    tq, tk, hg = _tile(s_len, 128), _tile(s_len, 512), 4

    outs = [[] for _ in range(8)]
    for l in range(depth):
        w = _layer_weights(l, w_in, w_conv, b_conv, w_gate_a, b_gate_a, w_gate_x, b_gate_x, lru_lambda, g_q,
                           w_uq, g_kv, w_uk, w_uv, w_o, w_up, w_down, g_pre_mix, g_post_mix, g_pre_mlp,
                           g_post_mlp)
        mod = _ada(c_all, w_ada[l].astype(BF16), b_ada[l].reshape(1, -1))
        mod_p = [mod[0:1, k * D_MODEL:(k + 1) * D_MODEL] for k in range(6)]
        mod_s = [mod[8:, k * D_MODEL:(k + 1) * D_MODEL] for k in range(6)]

        yg, h_last, x_tail, g_att, q_lat, q_pe, lat, lat_b, kpe, lat_t, kpe_t = _inproj(
            xp, mod_p[0], mod_p[1], w, tabs_p, tm_in, scan=True)
        merged = _attn_prompt(q_lat, q_pe, lat_t, kpe_t, lat_b, g_att, yg, w, tq, tk, hg)
        xp = _outmlp(merged, xp, mod_p[2], mod_p[3], mod_p[4], mod_p[5], w, tm_out, 1024)
        outs[0].append(lat.reshape(1, s_len, KV_LORA))
        outs[1].append(kpe.reshape(1, s_len, QK_ROPE))
        outs[2].append(x_tail[SUBLANES - (CONV_W - 1):].reshape(1, CONV_W - 1, D_RNN))
        outs[3].append(h_last)

        xr, g_rnn, g_att, q_lat, q_pe, lat, _, kpe, _, _ = _inproj(
            xs, mod_s[0], mod_s[1], w, tabs_s, bd, scan=False)
        conv_rows = [state_conv[l][:, k, :] for k in range(CONV_W - 1)]
        yg, h_new = _rglru_sample(xr, g_rnn, conv_rows, state_rnn[l], w, at_start=(past_len == 0))
        pool_kpe_t = jnp.swapaxes(cache_krope[l:l + 1], 2, 3)
        o = _attn_sample(q_lat, q_pe, lat, kpe, cache_latent[l:l + 1], pool_kpe_t, page_table, pages, streams)
        merged = _merge_sample(o, g_att, yg, w)
        xs = _outmlp(merged, xs, mod_s[2], mod_s[3], mod_s[4], mod_s[5], w, bd, 1024)
        outs[4].append(lat.reshape(bd, 1, KV_LORA))
        outs[5].append(kpe.reshape(bd, 1, QK_ROPE))
        outs[6].append(jnp.stack(conv_rows[1:] + [xr], axis=1))
        outs[7].append(h_new)

    return (xp.reshape(1, s_len, D_MODEL), xs.reshape(bd, 1, D_MODEL)) + tuple(jnp.stack(o) for o in outs)
```
